```python
import math
import jax, jax.numpy as jnp
from jax import lax
import numpy as np

D_MODEL = 1024
BATCH = 2
SEQ = 8192
DEPTH = 2

MEM_LEN = 256
EPS = 1e-6
MLA_HEADS = 8
Q_LORA = 384
KV_LORA = 256
D_NOPE = 64
D_ROPE = 32
D_QK = D_NOPE + D_ROPE
D_V = 64
MLA_WIDTH = MLA_HEADS * D_V
ROPE_THETA = 10000.0
Q_BLOCK = 128
SSM_GROUPS = 32
SSM_GROUP_CH = 16
SSM_WIDTH = SSM_GROUPS * SSM_GROUP_CH
SSM_STATE = 64
DT_MIN = 1e-3
DT_MAX = 1e-1
X_HEADS = 4
X_HEAD_DIM = 128
X_WIDTH = X_HEADS * X_HEAD_DIM
N_BRANCH = 3
D_FF = 2816
CONV_WIDTH = 3
IN_WIDTH = Q_LORA + KV_LORA + D_ROPE + SSM_WIDTH + X_WIDTH + N_BRANCH * D_MODEL

kernel_name = "hybrid_mla_s5_memxattn_convffn"


def rmsnorm(x, g):
    xf = x.astype(jnp.float32)
    y = xf * lax.rsqrt(jnp.mean(xf * xf, axis=-1, keepdims=True) + EPS)
    return (y * g.astype(jnp.float32)).astype(x.dtype)


def rope_tables(positions):
    inv_freq = ROPE_THETA ** (-jnp.arange(0, D_ROPE, 2, dtype=jnp.float32) / D_ROPE)
    ang = positions.astype(jnp.float32)[..., None] * inv_freq
    return jnp.cos(ang), jnp.sin(ang)


def apply_rope(x, cos, sin):
    xf = x.astype(jnp.float32)
    x1, x2 = jnp.split(xf, 2, axis=-1)
    return jnp.concatenate([x1 * cos - x2 * sin, x1 * sin + x2 * cos], axis=-1).astype(x.dtype)


def split_combined(proj):
    sizes = (Q_LORA, KV_LORA, D_ROPE, SSM_WIDTH, X_WIDTH)
    idx = [int(v) for v in np.cumsum(sizes)]
    return jnp.split(proj, idx, axis=-1)


def causal_block_attention(q, k, v):
    b, l, h, dq = q.shape
    nb = l // Q_BLOCK
    scale = dq ** -0.5
    qb = jnp.moveaxis(q.reshape(b, nb, Q_BLOCK, h, dq), 1, 0)
    kpos = jnp.arange(l)

    def one_block(args):
        i, qi = args
        s = jnp.einsum('bqhd,bkhd->bhqk', qi, k, preferred_element_type=jnp.float32) * scale
        qpos = i * Q_BLOCK + jnp.arange(Q_BLOCK)
        s = jnp.where(kpos[None, :] <= qpos[:, None], s, -jnp.inf)
        p = jax.nn.softmax(s, axis=-1)
        return jnp.einsum('bhqk,bkhd->bqhd', p.astype(v.dtype), v)

    out = lax.map(one_block, (jnp.arange(nb), qb))
    return jnp.moveaxis(out, 0, 1).reshape(b, l, h * v.shape[-1])


def mla_branch(c_q, c_kv, k_r, cos, sin, q_a_norm_g, w_q_b, kv_a_norm_g, w_kv_b, q_norm_g, k_norm_g):
    b, l, _ = c_q.shape
    q = (rmsnorm(c_q, q_a_norm_g) @ w_q_b).reshape(b, l, MLA_HEADS, D_QK)
    kv = (rmsnorm(c_kv, kv_a_norm_g) @ w_kv_b).reshape(b, l, MLA_HEADS, D_NOPE + D_V)
    k_nope, v = kv[..., :D_NOPE], kv[..., D_NOPE:]
    k_rope = jnp.broadcast_to(k_r[:, :, None, :], (b, l, MLA_HEADS, D_ROPE))
    k = jnp.concatenate([k_nope, k_rope], axis=-1)
    q = rmsnorm(q, q_norm_g)
    k = rmsnorm(k, k_norm_g)
    c4, s4 = cos[:, :, None, :], sin[:, :, None, :]
    q = jnp.concatenate([q[..., :D_NOPE], apply_rope(q[..., D_NOPE:], c4, s4)], axis=-1)
    k = jnp.concatenate([k[..., :D_NOPE], apply_rope(k[..., D_NOPE:], c4, s4)], axis=-1)
    return causal_block_attention(q, k, v)


def _ssm_combine(left, right):
    a1r, a1i, b1r, b1i = left
    a2r, a2i, b2r, b2i = right
    return (a2r * a1r - a2i * a1i,
            a2r * a1i + a2i * a1r,
            a2r * b1r - a2i * b1i + b2r,
            a2r * b1i + a2i * b1r + b2i)


def s5_branch(u, lam_re, lam_im, log_dt, b_re, b_im, c_re, c_im, d_skip, w_glu, b_glu):
    f32 = jnp.float32
    bsz, l, _ = u.shape
    uf = u.astype(f32).reshape(bsz, l, SSM_GROUPS, SSM_GROUP_CH)
    dt = jnp.exp(log_dt.astype(f32))[:, None]
    lr, li = lam_re.astype(f32), lam_im.astype(f32)
    mag = jnp.exp(lr * dt)
    a_re, a_im = mag * jnp.cos(li * dt), mag * jnp.sin(li * dt)
    den = lr * lr + li * li
    e_re, e_im = a_re - 1.0, a_im
    f_re = ((e_re * lr + e_im * li) / den)[..., None]
    f_im = ((e_im * lr - e_re * li) / den)[..., None]
    br, bi = b_re.astype(f32), b_im.astype(f32)
    bb_re = f_re * br - f_im * bi
    bb_im = f_re * bi + f_im * br
    bu_re = jnp.einsum('blgc,gnc->blgn', uf, bb_re)
    bu_im = jnp.einsum('blgc,gnc->blgn', uf, bb_im)
    a_re_t = jnp.broadcast_to(a_re, bu_re.shape)
    a_im_t = jnp.broadcast_to(a_im, bu_im.shape)
    _, _, s_re, s_im = lax.associative_scan(_ssm_combine, (a_re_t, a_im_t, bu_re, bu_im), axis=1)
    y = (jnp.einsum('blgn,gcn->blgc', s_re, c_re.astype(f32))
         - jnp.einsum('blgn,gcn->blgc', s_im, c_im.astype(f32))
         + d_skip.astype(f32) * uf)
    y = jax.nn.gelu(y.reshape(bsz, l, SSM_WIDTH)).astype(u.dtype)
    return y * jax.nn.sigmoid(y @ w_glu + b_glu)


def cross_branch(x_q, mem, mem_norm_g, w_mem_kv, xq_norm_g, xk_norm_g):
    b, l, _ = x_q.shape
    kv = rmsnorm(mem, mem_norm_g) @ w_mem_kv
    k = kv[..., :X_WIDTH].reshape(b, MEM_LEN, X_HEADS, X_HEAD_DIM)
    v = kv[..., X_WIDTH:].reshape(b, MEM_LEN, X_HEADS, X_HEAD_DIM)
    q = rmsnorm(x_q.reshape(b, l, X_HEADS, X_HEAD_DIM), xq_norm_g)
    k = rmsnorm(k, xk_norm_g)
    s = jnp.einsum('blhd,bmhd->bhlm', q, k, preferred_element_type=jnp.float32) * (X_HEAD_DIM ** -0.5)
    p = jax.nn.softmax(s, axis=-1)
    return jnp.einsum('bhlm,bmhd->blhd', p.astype(v.dtype), v).reshape(b, l, X_WIDTH)


def causal_dwconv(x, w, bias):
    c = x.shape[-1]
    y = lax.conv_general_dilated(x, w[:, None, :].astype(x.dtype), window_strides=(1,),
                                 padding=((CONV_WIDTH - 1, 0),),
                                 dimension_numbers=('NWC', 'WIO', 'NWC'),
                                 feature_group_count=c)
    return y + bias


def setup_inputs(seed: int = 0) -> dict:
    key = jax.random.key(seed)
    ks = iter(jax.random.split(key, 48))
    f32 = jnp.float32
    L = DEPTH

    def nrm(shape, fan_in):
        return jax.random.normal(next(ks), shape, f32) * (fan_in ** -0.5)

    def gain(shape):
        return 1.0 + 0.02 * jax.random.normal(next(ks), shape, f32)

    def small(shape):
        return 0.01 * jax.random.normal(next(ks), shape, f32)

    x = jax.random.normal(next(ks), (BATCH, SEQ, D_MODEL), f32)
    mem = jax.random.normal(next(ks), (BATCH, MEM_LEN, D_MODEL), f32)
    offset = jax.random.randint(next(ks), (BATCH, 1), 0, 1024, dtype=jnp.int32)
    positions = offset + jnp.arange(SEQ, dtype=jnp.int32)[None, :]
    n_idx = jnp.arange(SSM_STATE, dtype=f32)
    lam_re = -0.5 + small((L, SSM_GROUPS, SSM_STATE))
    lam_im = math.pi * n_idx + small((L, SSM_GROUPS, SSM_STATE))
    log_dt = jax.random.uniform(next(ks), (L, SSM_GROUPS), f32, math.log(DT_MIN), math.log(DT_MAX))
    return {
        "x": x,
        "mem": mem,
        "positions": positions,
        "norm_mix_g": gain((L, D_MODEL)),
        "w_in": nrm((L, D_MODEL, IN_WIDTH), D_MODEL),
        "q_a_norm_g": gain((L, Q_LORA)),
        "w_q_b": nrm((L, Q_LORA, MLA_HEADS * D_QK), Q_LORA),
        "kv_a_norm_g": gain((L, KV_LORA)),
        "w_kv_b": nrm((L, KV_LORA, MLA_HEADS * (D_NOPE + D_V)), KV_LORA),
        "q_norm_g": gain((L, D_QK)),
        "k_norm_g": gain((L, D_QK)),
        "w_o_mla": nrm((L, MLA_WIDTH, D_MODEL), MLA_WIDTH),
        "ssm_lambda_re": lam_re,
        "ssm_lambda_im": lam_im,
        "ssm_log_dt": log_dt,
        "ssm_b_re": nrm((L, SSM_GROUPS, SSM_STATE, SSM_GROUP_CH), 2 * SSM_GROUP_CH),
        "ssm_b_im": nrm((L, SSM_GROUPS, SSM_STATE, SSM_GROUP_CH), 2 * SSM_GROUP_CH),
        "ssm_c_re": nrm((L, SSM_GROUPS, SSM_GROUP_CH, SSM_STATE), SSM_STATE),
        "ssm_c_im": nrm((L, SSM_GROUPS, SSM_GROUP_CH, SSM_STATE), SSM_STATE),
        "ssm_d": jax.random.normal(next(ks), (L, SSM_GROUPS, SSM_GROUP_CH), f32),
        "w_glu": nrm((L, SSM_WIDTH, SSM_WIDTH), SSM_WIDTH),
        "b_glu": small((L, SSM_WIDTH)),
        "w_o_ssm": nrm((L, SSM_WIDTH, D_MODEL), SSM_WIDTH),
        "mem_norm_g": gain((L, D_MODEL)),
        "w_mem_kv": nrm((L, D_MODEL, 2 * X_WIDTH), D_MODEL),
        "xq_norm_g": gain((L, X_HEAD_DIM)),
        "xk_norm_g": gain((L, X_HEAD_DIM)),
        "w_o_cross": nrm((L, X_WIDTH, D_MODEL), X_WIDTH),
        "b_gate": small((L, N_BRANCH * D_MODEL)),
        "w_out": nrm((L, D_MODEL, D_MODEL), D_MODEL),
        "norm_ffn_g": gain((L, D_MODEL)),
        "w_up": nrm((L, D_MODEL, 2 * D_FF), D_MODEL),
        "conv_w": nrm((L, CONV_WIDTH, 2 * D_FF), CONV_WIDTH),
        "conv_b": small((L, 2 * D_FF)),
        "w_down": nrm((L, D_FF, D_MODEL), D_FF),
    }


def reference(x, mem, positions, norm_mix_g, w_in, q_a_norm_g, w_q_b, kv_a_norm_g, w_kv_b,
              q_norm_g, k_norm_g, w_o_mla, ssm_lambda_re, ssm_lambda_im, ssm_log_dt,
              ssm_b_re, ssm_b_im, ssm_c_re, ssm_c_im, ssm_d, w_glu, b_glu, w_o_ssm,
              mem_norm_g, w_mem_kv, xq_norm_g, xk_norm_g, w_o_cross, b_gate, w_out,
              norm_ffn_g, w_up, conv_w, conv_b, w_down):
    bsz, l, _ = x.shape
    cos, sin = rope_tables(positions)
    for i in range(DEPTH):
        h = rmsnorm(x, norm_mix_g[i])
        c_q, c_kv, k_r, u_ssm, x_q, gate_logits = split_combined(h @ w_in[i])
        y_a = mla_branch(c_q, c_kv, k_r, cos, sin, q_a_norm_g[i], w_q_b[i], kv_a_norm_g[i],
                         w_kv_b[i], q_norm_g[i], k_norm_g[i]) @ w_o_mla[i]
        y_b = s5_branch(u_ssm, ssm_lambda_re[i], ssm_lambda_im[i], ssm_log_dt[i], ssm_b_re[i],
                        ssm_b_im[i], ssm_c_re[i], ssm_c_im[i], ssm_d[i], w_glu[i], b_glu[i]) @ w_o_ssm[i]
        y_c = cross_branch(x_q, mem, mem_norm_g[i], w_mem_kv[i], xq_norm_g[i], xk_norm_g[i]) @ w_o_cross[i]
        gates = jax.nn.sigmoid(gate_logits + b_gate[i]).reshape(bsz, l, N_BRANCH, D_MODEL)
        merged = gates[:, :, 0] * y_a + gates[:, :, 1] * y_b + gates[:, :, 2] * y_c
        x = x + merged @ w_out[i]
        h2 = rmsnorm(x, norm_ffn_g[i])
        up = causal_dwconv(h2 @ w_up[i], conv_w[i], conv_b[i])
        g_ff, v_ff = up[..., :D_FF], up[..., D_FF:]
        x = x + (jax.nn.silu(g_ff) * v_ff) @ w_down[i]
    return x
```

```python
import functools
import math

import jax
import jax.numpy as jnp
from jax import lax
from jax.experimental import pallas as pl
from jax.experimental.pallas import tpu as pltpu

F32 = jnp.float32
BF16 = jnp.bfloat16

D_MODEL = 1024
DEPTH = 2
MEM_LEN = 256
EPS = 1e-6
MLA_HEADS = 8
Q_LORA = 384
KV_LORA = 256
D_NOPE = 64
D_ROPE = 32
D_QK = D_NOPE + D_ROPE
D_V = 64
MLA_WIDTH = MLA_HEADS * D_V
ROPE_THETA = 10000.0
SSM_GROUPS = 32
SSM_GROUP_CH = 16
SSM_WIDTH = SSM_GROUPS * SSM_GROUP_CH
SSM_STATE = 64
SSM_NS = SSM_GROUPS * SSM_STATE
X_HEADS = 4
X_HEAD_DIM = 128
X_WIDTH = X_HEADS * X_HEAD_DIM
N_BRANCH = 3
D_FF = 2816
CONV_WIDTH = 3

LANES = 128
SUBLANES = 8
HEAD_PAD = LANES

TM_FRONT = 512
TM_MERGE = 512
TM_FFN = 512
FFN_CHUNK = 256
TQ = 256
TK = 256
S5_SEG = 32
S5_CHUNK = SUBLANES * S5_SEG
S5_COLS = 512
ROPE_ROWS = 1024

VMEM_LIMIT = 56 * 1024 * 1024


def _const_spec(shape):
    nd = len(shape)
    return pl.BlockSpec(shape, lambda *_: (0,) * nd)


def _rms(x, g):
    return x * lax.rsqrt(jnp.mean(x * x, axis=-1, keepdims=True) + EPS) * g


def _dot(a, b):
    return jnp.dot(a, b, preferred_element_type=F32)


def _dot_nt(a, b):
    return lax.dot_general(a, b, (((1,), (1,)), ((), ())), preferred_element_type=F32)


def _params(sem, limit=VMEM_LIMIT):
    return pltpu.CompilerParams(dimension_semantics=sem, vmem_limit_bytes=limit)


def _rope_kernel(pos_ref, invf_ref, cc_ref, sa_ref, sb_ref):
    ang = pos_ref[...].astype(F32) * invf_ref[...]
    c = jnp.cos(ang)
    s = jnp.sin(ang)
    lane = lax.broadcasted_iota(jnp.int32, ang.shape, 1)
    half = D_ROPE // 2
    cc_ref[...] = jnp.where(lane < D_QK, c, 0.0)
    sa_ref[...] = jnp.where((lane >= D_NOPE) & (lane < D_NOPE + half), -s, 0.0)
    sb_ref[...] = jnp.where((lane >= D_NOPE + half) & (lane < D_QK), s, 0.0)


def _rope_tables(pos_col, invf):
    t = pos_col.shape[0]
    out = jax.ShapeDtypeStruct((t, HEAD_PAD), F32)
    spec = pl.BlockSpec((ROPE_ROWS, HEAD_PAD), lambda i: (i, 0))
    return pl.pallas_call(
        _rope_kernel,
        grid=(t // ROPE_ROWS,),
        in_specs=[pl.BlockSpec((ROPE_ROWS, 1), lambda i: (i, 0)), _const_spec((1, HEAD_PAD))],
        out_specs=[spec, spec, spec],
        out_shape=[out, out, out],
        compiler_params=_params(("arbitrary",)),
        name="rope_tables",
    )(pos_col, invf)


def _memkv_kernel(mem_ref, g_ref, w_ref, gk_ref, k_ref, v_ref):
    m = _rms(mem_ref[0], g_ref[...]).astype(BF16)
    kv = _dot(m, w_ref[...])
    for h in range(X_HEADS):
        sl = slice(h * X_HEAD_DIM, (h + 1) * X_HEAD_DIM)
        k_ref[0, :, sl] = _rms(kv[:, sl], gk_ref[...]).astype(BF16)
    v_ref[0] = kv[:, X_WIDTH:].astype(BF16)


def _memkv(mem, g, w, gk):
    b = mem.shape[0]
    out = jax.ShapeDtypeStruct((b, MEM_LEN, X_WIDTH), BF16)
    spec = pl.BlockSpec((1, MEM_LEN, X_WIDTH), lambda i: (i, 0, 0))
    return pl.pallas_call(
        _memkv_kernel,
        grid=(b,),
        in_specs=[pl.BlockSpec((1, MEM_LEN, D_MODEL), lambda i: (i, 0, 0)),
                  _const_spec(g.shape), _const_spec(w.shape), _const_spec(gk.shape)],
        out_specs=[spec, spec],
        out_shape=[out, out],
        compiler_params=_params(("arbitrary",)),
        name="mem_kv",
    )(mem, g, w, gk)


def _front_kernel(x_ref, cc_ref, sa_ref, sb_ref, kx_ref, vx_ref, gmix_ref, win_ref,
                  gqa_ref, wq_ref, gkva_ref, wk_ref, wv_ref, gq_ref, gk_ref, gxq_ref,
                  q_ref, k_ref, v_ref, u_ref, yc_ref):
    h = _rms(x_ref[...], gmix_ref[...]).astype(BF16)
    p = _dot(h, win_ref[...])
    o_ckv = Q_LORA
    o_kr = o_ckv + KV_LORA
    o_u = o_kr + HEAD_PAD
    o_xq = o_u + SSM_WIDTH
    cq = p[:, :o_ckv]
    ckv = p[:, o_ckv:o_kr]
    kr = p[:, o_kr:o_u]
    u_ref[...] = p[:, o_u:o_xq].astype(BF16)
    xq = p[:, o_xq:]

    cqn = _rms(cq, gqa_ref[...]).astype(BF16)
    ckvn = _rms(ckv, gkva_ref[...]).astype(BF16)
    v_ref[...] = _dot(ckvn, wv_ref[...]).astype(BF16)

    cc = cc_ref[...]
    sa = sa_ref[...]
    sb = sb_ref[...]
    half = D_ROPE // 2

    def head_norm_rope(v, g):
        ms = jnp.sum(v * v, axis=-1, keepdims=True) * (1.0 / D_QK)
        v = v * lax.rsqrt(ms + EPS) * g
        return v * cc + pltpu.roll(v, HEAD_PAD - half, 1) * sa + pltpu.roll(v, half, 1) * sb

    scale = D_QK ** -0.5
    for hd in range(MLA_HEADS):
        qh = head_norm_rope(_dot(cqn, wq_ref[hd]), gq_ref[...])
        q_ref[0, hd] = (qh * scale).astype(BF16)
        kh = head_norm_rope(_dot(ckvn, wk_ref[hd]) + kr, gk_ref[...])
        k_ref[0, hd] = kh.astype(BF16)

    xscale = X_HEAD_DIM ** -0.5
    for hd in range(X_HEADS):
        sl = slice(hd * X_HEAD_DIM, (hd + 1) * X_HEAD_DIM)
        qx = (_rms(xq[:, sl], gxq_ref[...]) * xscale).astype(BF16)
        s = _dot_nt(qx, kx_ref[0, :, sl])
        pm = jnp.exp(s - jnp.max(s, axis=-1, keepdims=True))
        l = jnp.sum(pm, axis=-1, keepdims=True)
        yc_ref[:, sl] = (_dot(pm.astype(BF16), vx_ref[0, :, sl]) / l).astype(BF16)


def _front(x2, tables, kx, vx, w, bsz, seq):
    t = bsz * seq
    tm = TM_FRONT
    nb = seq // tm
    row = lambda c: pl.BlockSpec((tm, c), lambda i: (i, 0))
    hspec = pl.BlockSpec((1, MLA_HEADS, tm, HEAD_PAD), lambda i: (i // nb, 0, i % nb, 0))
    mspec = pl.BlockSpec((1, MEM_LEN, X_WIDTH), lambda i: (i // nb, 0, 0))
    consts = [w["gmix"], w["win"], w["gqa"], w["wq"], w["gkva"], w["wk"], w["wv"],
              w["gq"], w["gk"], w["gxq"]]
    hshape = jax.ShapeDtypeStruct((bsz, MLA_HEADS, seq, HEAD_PAD), BF16)
    tshape = jax.ShapeDtypeStruct((t, MLA_WIDTH), BF16)
    return pl.pallas_call(
        _front_kernel,
        grid=(t // tm,),
        in_specs=[row(D_MODEL), row(HEAD_PAD), row(HEAD_PAD), row(HEAD_PAD), mspec, mspec]
                 + [_const_spec(c.shape) for c in consts],
        out_specs=[hspec, hspec, row(MLA_WIDTH), row(SSM_WIDTH), row(X_WIDTH)],
        out_shape=[hshape, hshape, tshape, tshape, tshape],
        compiler_params=_params(("arbitrary",)),
        name="front",
    )(x2, *tables, kx, vx, *consts)


def _attn_kernel(q_ref, k_ref, v_ref, o_ref, m_scr, l_scr, acc_scr):
    qi = pl.program_id(2)
    m_scr[...] = jnp.full(m_scr.shape, -jnp.inf, F32)
    l_scr[...] = jnp.zeros(l_scr.shape, F32)
    acc_scr[...] = jnp.zeros(acc_scr.shape, F32)

    def step(kb, masked):
        r = pl.multiple_of(kb * TK, TK)
        vblk = v_ref[0, pl.ds(r, TK), :]
        for hh in range(2):
            s = _dot_nt(q_ref[0, hh], k_ref[0, hh, pl.ds(r, TK), :])
            if masked:
                row = lax.broadcasted_iota(jnp.int32, s.shape, 0)
                col = lax.broadcasted_iota(jnp.int32, s.shape, 1)
                s = jnp.where(col <= row, s, -jnp.inf)
            m_old = m_scr[hh]
            m_new = jnp.maximum(m_old, jnp.max(s, axis=-1, keepdims=True))
            pm = jnp.exp(s - m_new)
            alpha = jnp.exp(m_old - m_new)
            l_scr[hh] = alpha * l_scr[hh] + jnp.sum(pm, axis=-1, keepdims=True)
            acc_scr[hh] = alpha * acc_scr[hh] + _dot(pm.astype(BF16), vblk)
            m_scr[hh] = m_new

    def body(kb, c):
        step(kb, False)
        return c

    lax.fori_loop(0, qi, body, 0)
    step(qi, True)
    o0 = acc_scr[0] / l_scr[0]
    o1 = acc_scr[1] / l_scr[1]
    lane = lax.broadcasted_iota(jnp.int32, o0.shape, 1)
    o_ref[0] = jnp.where(lane < D_V, o0, o1).astype(BF16)


def _attention(q, k, v3):
    bsz, nh, seq, _ = q.shape
    assert TQ == TK
    pair = 2 * D_V
    return pl.pallas_call(
        _attn_kernel,
        grid=(bsz, nh // 2, seq // TQ),
        in_specs=[pl.BlockSpec((1, 2, TQ, HEAD_PAD), lambda b, hp, i: (b, hp, i, 0)),
                  pl.BlockSpec((1, 2, seq, HEAD_PAD), lambda b, hp, i: (b, hp, 0, 0)),
                  pl.BlockSpec((1, seq, pair), lambda b, hp, i: (b, 0, hp))],
        out_specs=pl.BlockSpec((1, TQ, pair), lambda b, hp, i: (b, i, hp)),
        out_shape=jax.ShapeDtypeStruct((bsz, seq, nh * D_V), BF16),
        scratch_shapes=[pltpu.VMEM((2, TQ, 1), F32), pltpu.VMEM((2, TQ, 1), F32),
                        pltpu.VMEM((2, TQ, pair), F32)],
        compiler_params=_params(("arbitrary", "arbitrary", "arbitrary")),
        name="mla_attention",
    )(q, k, v3)


def _s5_prep_kernel(lr_ref, li_ref, ldt_ref, lrc_ref, lic_ref, ldtc_ref, bre_ref, bim_ref,
                    a_ref, ptab_ref, bbre_ref, bbim_ref):
    ns = SSM_NS
    dt = jnp.exp(ldt_ref[0])
    lr = lr_ref[0] * dt
    li = li_ref[0] * dt
    k = (lax.broadcasted_iota(jnp.int32, (S5_SEG, 1), 0) + 1).astype(F32)
    mag = jnp.exp(k * lr)
    ang = k * li
    ptab_ref[0, :, :ns] = mag * jnp.cos(ang)
    ptab_ref[0, :, ns:] = mag * jnp.sin(ang)
    mag1 = jnp.exp(lr)
    a_ref[0, :, :ns] = mag1 * jnp.cos(li)
    a_ref[0, :, ns:] = mag1 * jnp.sin(li)
    dtc = jnp.exp(ldtc_ref[0])
    lrc = lrc_ref[0]
    lic = lic_ref[0]
    magc = jnp.exp(lrc * dtc)
    e_re = magc * jnp.cos(lic * dtc) - 1.0
    e_im = magc * jnp.sin(lic * dtc)
    den = lrc * lrc + lic * lic
    f_re = (e_re * lrc + e_im * lic) / den
    f_im = (e_im * lrc - e_re * lic) / den
    bre = bre_ref[0]
    bim = bim_ref[0]
    bbre_ref[0] = f_re * bre - f_im * bim
    bbim_ref[0] = f_re * bim + f_im * bre


def _s5_prep(lam_re, lam_im, log_dt, b_re, b_im):
    depth = lam_re.shape[0]
    ns = SSM_NS
    ldt = jnp.broadcast_to(log_dt[:, :, None], lam_re.shape)
    rowv = lambda a: a.reshape(depth, 1, ns)
    colv = lambda a: a.reshape(depth, ns, 1)
    bcol = lambda a: a.reshape(depth, ns, SSM_GROUP_CH)
    spec = lambda s: pl.BlockSpec((1,) + s, lambda i: (i, 0, 0))
    return pl.pallas_call(
        _s5_prep_kernel,
        grid=(depth,),
        in_specs=[spec((1, ns))] * 3 + [spec((ns, 1))] * 3 + [spec((ns, SSM_GROUP_CH))] * 2,
        out_specs=[spec((1, 2 * ns)), spec((S5_SEG, 2 * ns)),
                   spec((ns, SSM_GROUP_CH)), spec((ns, SSM_GROUP_CH))],
        out_shape=[jax.ShapeDtypeStruct((depth, 1, 2 * ns), F32),
                   jax.ShapeDtypeStruct((depth, S5_SEG, 2 * ns), F32),
                   jax.ShapeDtypeStruct((depth, ns, SSM_GROUP_CH), F32),
                   jax.ShapeDtypeStruct((depth, ns, SSM_GROUP_CH), F32)],
        compiler_params=_params(("arbitrary",)),
        name="s5_prep",
    )(rowv(lam_re), rowv(lam_im), rowv(ldt), colv(lam_re), colv(lam_im), colv(ldt),
      bcol(b_re), bcol(b_im))


def _s5_kernel(u_ref, bmat_ref, cre_ref, cim_ref, a_ref, ptab_ref, d_ref, wglu_ref, bglu_ref,
               o_ref, bu_scr, carry_scr, c_scr):
    ns = SSM_NS
    cw = S5_COLS

    @pl.when(pl.program_id(1) == 0)
    def _():
        carry_scr[...] = jnp.zeros(carry_scr.shape, F32)

    u = u_ref[0]
    bu_scr[...] = _dot(u, bmat_ref[...])

    for cg in range(ns // cw):
        re = slice(cg * cw, (cg + 1) * cw)
        im = slice(ns + cg * cw, ns + (cg + 1) * cw)
        a_re = jnp.broadcast_to(a_ref[:, re], (SUBLANES, cw))
        a_im = jnp.broadcast_to(a_ref[:, im], (SUBLANES, cw))

        def scan(i, z):
            z_re, z_im = z
            rows = pl.ds(pl.multiple_of(i * SUBLANES, SUBLANES), SUBLANES)
            n_re = a_re * z_re - a_im * z_im + bu_scr[rows, re]
            n_im = a_re * z_im + a_im * z_re + bu_scr[rows, im]
            bu_scr[rows, re] = n_re
            bu_scr[rows, im] = n_im
            return n_re, n_im

        zero = jnp.zeros((SUBLANES, cw), F32)
        e_re, e_im = lax.fori_loop(0, S5_SEG, scan, (zero, zero), unroll=4)

        p_re = ptab_ref[S5_SEG - 1:S5_SEG, re]
        p_im = ptab_ref[S5_SEG - 1:S5_SEG, im]
        cur_re = carry_scr[:, re]
        cur_im = carry_scr[:, im]
        for j in range(SUBLANES):
            c_scr[j:j + 1, re] = cur_re
            c_scr[j:j + 1, im] = cur_im
            n_re = e_re[j:j + 1, :] + p_re * cur_re - p_im * cur_im
            n_im = e_im[j:j + 1, :] + p_re * cur_im + p_im * cur_re
            cur_re, cur_im = n_re, n_im
        carry_scr[:, re] = cur_re
        carry_scr[:, im] = cur_im
        c_re = c_scr[:, re]
        c_im = c_scr[:, im]

        def fix(i, carry):
            rows = pl.ds(pl.multiple_of(i * SUBLANES, SUBLANES), SUBLANES)
            q_re = ptab_ref[pl.ds(i, 1), re]
            q_im = ptab_ref[pl.ds(i, 1), im]
            bu_scr[rows, re] = bu_scr[rows, re] + q_re * c_re - q_im * c_im
            bu_scr[rows, im] = bu_scr[rows, im] + q_re * c_im + q_im * c_re
            return carry

        lax.fori_loop(0, S5_SEG, fix, 0, unroll=4)

    y = (_dot(bu_scr[:, :ns].astype(BF16), cre_ref[...])
         - _dot(bu_scr[:, ns:].astype(BF16), cim_ref[...])
         + d_ref[...] * u.astype(F32))
    y = jax.nn.gelu(y)
    z = _dot(y.astype(BF16), wglu_ref[...]) + bglu_ref[...]
    o_ref[0] = (y * jax.nn.sigmoid(z)).astype(BF16)


def _s5(u_perm, bmat, cre, cim, a, ptab, d, wglu, bglu):
    bsz, seq, _ = u_perm.shape
    ns = SSM_NS
    blk = pl.BlockSpec((1, S5_CHUNK, SSM_WIDTH), lambda b, j: (b, j, 0))
    consts = [bmat, cre, cim, a, ptab, d, wglu, bglu]
    return pl.pallas_call(
        _s5_kernel,
        grid=(bsz, seq // S5_CHUNK),
        in_specs=[blk] + [_const_spec(c.shape) for c in consts],
        out_specs=blk,
        out_shape=jax.ShapeDtypeStruct((bsz, seq, SSM_WIDTH), BF16),
        scratch_shapes=[pltpu.VMEM((S5_CHUNK, 2 * ns), F32), pltpu.VMEM((1, 2 * ns), F32),
                        pltpu.VMEM((SUBLANES, 2 * ns), F32)],
        compiler_params=_params(("arbitrary", "arbitrary")),
        name="s5_scan",
    )(u_perm, *consts)


def _merge_kernel(x_ref, ya_ref, yb_ref, yc_ref, gmix_ref, wg_ref, bg_ref, woa_ref, wob_ref,
                  woc_ref, wout_ref, o_ref):
    x = x_ref[...]
    h = _rms(x, gmix_ref[...]).astype(BF16)
    merged = None
    for br, (y_ref, wo_ref) in enumerate(((ya_ref, woa_ref), (yb_ref, wob_ref), (yc_ref, woc_ref))):
        sl = slice(br * D_MODEL, (br + 1) * D_MODEL)
        gate = jax.nn.sigmoid(_dot(h, wg_ref[:, sl]) + bg_ref[:, sl])
        term = gate * _dot(y_ref[...], wo_ref[...])
        merged = term if merged is None else merged + term
    o_ref[...] = x + _dot(merged.astype(BF16), wout_ref[...])


def _merge(x2, ya, yb, yc, w):
    t = x2.shape[0]
    tm = TM_MERGE
    row = lambda c: pl.BlockSpec((tm, c), lambda i: (i, 0))
    consts = [w["gmix"], w["wg"], w["bg"], w["woa"], w["wob"], w["woc"], w["wout"]]
    return pl.pallas_call(
        _merge_kernel,
        grid=(t // tm,),
        in_specs=[row(D_MODEL), row(MLA_WIDTH), row(SSM_WIDTH), row(X_WIDTH)]
                 + [_const_spec(c.shape) for c in consts],
        out_specs=row(D_MODEL),
        out_shape=jax.ShapeDtypeStruct((t, D_MODEL), F32),
        compiler_params=_params(("arbitrary",)),
        name="merge",
    )(x2, ya, yb, yc, *consts)


def _ffn_kernel(x_ref, g_ref, wug_ref, wuv_ref, cwg_ref, cwv_ref, cbg_ref, cbv_ref, wd_ref,
                o_ref, carry_g, carry_v, *, blocks_per_seq):
    tm = x_ref.shape[0]

    @pl.when(pl.program_id(0) % blocks_per_seq == 0)
    def _():
        carry_g[...] = jnp.zeros(carry_g.shape, F32)
        carry_v[...] = jnp.zeros(carry_v.shape, F32)

    x = x_ref[...]
    h2 = _rms(x, g_ref[...]).astype(BF16)
    row = lax.broadcasted_iota(jnp.int32, (tm, 1), 0)

    def conv(cs, w_ref, cw_ref, cb_ref, carry):
        up = _dot(h2, w_ref[:, cs])
        prev = carry[:, cs]
        carry[:, cs] = up[tm - SUBLANES:, :]
        p1 = prev[SUBLANES - 1:SUBLANES, :]
        p2 = prev[SUBLANES - 2:SUBLANES - 1, :]
        m1 = jnp.where(row == 0, p1, pltpu.roll(up, 1, 0))
        m2 = jnp.where(row == 0, p2, jnp.where(row == 1, p1, pltpu.roll(up, 2, 0)))
        return cw_ref[0:1, cs] * m2 + cw_ref[1:2, cs] * m1 + cw_ref[2:3, cs] * up + cb_ref[:, cs]

    acc = jnp.zeros((tm, D_MODEL), F32)
    for c in range(D_FF // FFN_CHUNK):
        cs = slice(c * FFN_CHUNK, (c + 1) * FFN_CHUNK)
        gv = conv(cs, wug_ref, cwg_ref, cbg_ref, carry_g)
        vv = conv(cs, wuv_ref, cwv_ref, cbv_ref, carry_v)
        act = (gv * jax.nn.sigmoid(gv) * vv).astype(BF16)
        acc = acc + _dot(act, wd_ref[cs, :])
    o_ref[...] = x + acc


def _ffn(x2, w, seq):
    t = x2.shape[0]
    tm = TM_FFN
    row = pl.BlockSpec((tm, D_MODEL), lambda i: (i, 0))
    consts = [w["gffn"], w["wug"], w["wuv"], w["cwg"], w["cwv"], w["cbg"], w["cbv"], w["wd"]]
    return pl.pallas_call(
        functools.partial(_ffn_kernel, blocks_per_seq=seq // tm),
        grid=(t // tm,),
        in_specs=[row] + [_const_spec(c.shape) for c in consts],
        out_specs=row,
        out_shape=jax.ShapeDtypeStruct((t, D_MODEL), F32),
        scratch_shapes=[pltpu.VMEM((SUBLANES, D_FF), F32), pltpu.VMEM((SUBLANES, D_FF), F32)],
        compiler_params=_params(("arbitrary",)),
        name="ffn",
    )(x2, *consts)


def _block_diag(blocks):
    g, r, c = blocks.shape
    eye = jnp.eye(g, dtype=bool)
    full = jnp.where(eye[:, None, :, None], blocks[:, :, None, :], jnp.zeros((), blocks.dtype))
    return full.reshape(g * r, g * c)


def _pad_lanes(a, lo, width=HEAD_PAD):
    pad = [(0, 0)] * (a.ndim - 1) + [(lo, width - lo - a.shape[-1])]
    return jnp.pad(a, pad)


def _layer_weights(i, p):
    w_in = p["w_in"][i]
    o = 0
    parts = {}
    for name, width in (("cq", Q_LORA), ("ckv", KV_LORA), ("kr", D_ROPE), ("u", SSM_WIDTH),
                        ("xq", X_WIDTH), ("g", N_BRANCH * D_MODEL)):
        parts[name] = w_in[:, o:o + width]
        o += width
    win = jnp.concatenate([parts["cq"], parts["ckv"], _pad_lanes(parts["kr"], D_NOPE),
                           parts["u"], parts["xq"]], axis=1).astype(BF16)
    wq = p["w_q_b"][i].reshape(Q_LORA, MLA_HEADS, D_QK).transpose(1, 0, 2)
    wkv = p["w_kv_b"][i].reshape(KV_LORA, MLA_HEADS, D_NOPE + D_V)
    wk = wkv[:, :, :D_NOPE].transpose(1, 0, 2)
    wv = wkv[:, :, D_NOPE:].reshape(KV_LORA, MLA_WIDTH)
    row = lambda a: a.reshape(1, -1)
    w_up = p["w_up"][i]
    conv_w = p["conv_w"][i]
    conv_b = p["conv_b"][i]
    return {
        "gmix": row(p["norm_mix_g"][i]),
        "win": win,
        "gqa": row(p["q_a_norm_g"][i]),
        "wq": _pad_lanes(wq, 0).astype(BF16),
        "gkva": row(p["kv_a_norm_g"][i]),
        "wk": _pad_lanes(wk, 0).astype(BF16),
        "wv": wv.astype(BF16),
        "gq": _pad_lanes(row(p["q_norm_g"][i]), 0),
        "gk": _pad_lanes(row(p["k_norm_g"][i]), 0),
        "gxq": row(p["xq_norm_g"][i]),
        "wg": parts["g"].astype(BF16),
        "bg": row(p["b_gate"][i]),
        "woa": p["w_o_mla"][i].astype(BF16),
        "wob": p["w_o_ssm"][i].astype(BF16),
        "woc": p["w_o_cross"][i].astype(BF16),
        "wout": p["w_out"][i].astype(BF16),
        "gffn": row(p["norm_ffn_g"][i]),
        "wug": w_up[:, :D_FF].astype(BF16),
        "wuv": w_up[:, D_FF:].astype(BF16),
        "cwg": conv_w[:, :D_FF],
        "cwv": conv_w[:, D_FF:],
        "cbg": row(conv_b[:D_FF]),
        "cbv": row(conv_b[D_FF:]),
        "wd": p["w_down"][i].astype(BF16),
    }


def _segment_major(a, bsz, seq):
    c = a.shape[-1]
    a = a.reshape(bsz, seq // S5_CHUNK, SUBLANES, S5_SEG, c)
    return a.transpose(0, 1, 3, 2, 4).reshape(bsz, seq, c)


def _time_major(a, bsz, seq):
    c = a.shape[-1]
    a = a.reshape(bsz, seq // S5_CHUNK, S5_SEG, SUBLANES, c)
    return a.transpose(0, 1, 3, 2, 4).reshape(bsz, seq, c)


def kernel(x, mem, positions, norm_mix_g, w_in, q_a_norm_g, w_q_b, kv_a_norm_g, w_kv_b, q_norm_g, k_norm_g, w_o_mla, ssm_lambda_re, ssm_lambda_im, ssm_log_dt, ssm_b_re, ssm_b_im, ssm_c_re, ssm_c_im, ssm_d, w_glu, b_glu, w_o_ssm, mem_norm_g, w_mem_kv, xq_norm_g, xk_norm_g, w_o_cross, b_gate, w_out, norm_ffn_g, w_up, conv_w, conv_b, w_down):
    p = dict(norm_mix_g=norm_mix_g, w_in=w_in, q_a_norm_g=q_a_norm_g, w_q_b=w_q_b,
             kv_a_norm_g=kv_a_norm_g, w_kv_b=w_kv_b, q_norm_g=q_norm_g, k_norm_g=k_norm_g,
             w_o_mla=w_o_mla, w_o_ssm=w_o_ssm, w_o_cross=w_o_cross, b_gate=b_gate, w_out=w_out,
             norm_ffn_g=norm_ffn_g, w_up=w_up, conv_w=conv_w, conv_b=conv_b, w_down=w_down,
             xq_norm_g=xq_norm_g)
    bsz, seq, _ = x.shape
    t = bsz * seq
    assert seq % max(TM_FRONT, TM_MERGE, TM_FFN, TQ, S5_CHUNK) == 0 and t % ROPE_ROWS == 0

    inv_freq = ROPE_THETA ** (-jnp.arange(0, D_ROPE, 2, dtype=F32) / D_ROPE)
    invf = jnp.concatenate([jnp.zeros((D_NOPE,), F32), inv_freq, inv_freq,
                            jnp.zeros((HEAD_PAD - D_QK,), F32)]).reshape(1, HEAD_PAD)
    tables = _rope_tables(positions.reshape(t, 1), invf)

    a_all, ptab_all, bbre_all, bbim_all = _s5_prep(ssm_lambda_re, ssm_lambda_im, ssm_log_dt,
                                                   ssm_b_re, ssm_b_im)

    x2 = x.reshape(t, D_MODEL)
    for i in range(DEPTH):
        w = _layer_weights(i, p)
        kx, vx = _memkv(mem, mem_norm_g[i].reshape(1, -1), w_mem_kv[i].astype(BF16),
                        xk_norm_g[i].reshape(1, -1))
        q, k, v, u, yc = _front(x2, tables, kx, vx, w, bsz, seq)
        ya = _attention(q, k, v.reshape(bsz, seq, MLA_WIDTH)).reshape(t, MLA_WIDTH)

        shp = (SSM_GROUPS, SSM_STATE, SSM_GROUP_CH)
        bmat = jnp.concatenate(
            [_block_diag(bbre_all[i].reshape(shp).transpose(0, 2, 1)),
             _block_diag(bbim_all[i].reshape(shp).transpose(0, 2, 1))], axis=1).astype(BF16)
        cre = _block_diag(ssm_c_re[i].transpose(0, 2, 1)).astype(BF16)
        cim = _block_diag(ssm_c_im[i].transpose(0, 2, 1)).astype(BF16)
        yb = _s5(_segment_major(u.reshape(bsz, seq, SSM_WIDTH), bsz, seq), bmat, cre, cim,
                 a_all[i], ptab_all[i], ssm_d[i].reshape(1, SSM_WIDTH), w_glu[i].astype(BF16),
                 b_glu[i].reshape(1, SSM_WIDTH))
        yb = _time_major(yb, bsz, seq).reshape(t, SSM_WIDTH)

        x2 = _merge(x2, ya, yb, yc, w)
        x2 = _ffn(x2, w, seq)
    return x2.reshape(bsz, seq, D_MODEL)
```

```python
import functools
import math

import jax
import jax.numpy as jnp
from jax import lax
from jax.experimental import pallas as pl
from jax.experimental.pallas import tpu as pltpu

F32 = jnp.float32
BF16 = jnp.bfloat16

D_MODEL = 1024
DEPTH = 2
MEM_LEN = 256
EPS = 1e-6
MLA_HEADS = 8
Q_LORA = 384
KV_LORA = 256
D_NOPE = 64
D_ROPE = 32
D_QK = D_NOPE + D_ROPE
D_V = 64
MLA_WIDTH = MLA_HEADS * D_V
ROPE_THETA = 10000.0
SSM_GROUPS = 32
SSM_GROUP_CH = 16
SSM_WIDTH = SSM_GROUPS * SSM_GROUP_CH
SSM_STATE = 64
SSM_NS = SSM_GROUPS * SSM_STATE
X_HEADS = 4
X_HEAD_DIM = 128
X_WIDTH = X_HEADS * X_HEAD_DIM
N_BRANCH = 3
D_FF = 2816
CONV_WIDTH = 3

LANES = 128
SUBLANES = 8
HEAD_PAD = LANES

TM_FRONT = 512
TM_MERGE = 512
TM_FFN = 512
FFN_CHUNK = 256
TQ = 512
TK = 256
ATTN_HEADS = 4
V_ROWS = D_V + 16
S5_SEG = 32
S5_CHUNK = SUBLANES * S5_SEG
S5_COLS = 512
ROPE_ROWS = 1024

VMEM_LIMIT = 56 * 1024 * 1024


def _const_spec(shape):
    nd = len(shape)
    return pl.BlockSpec(shape, lambda *_: (0,) * nd)


def _rms(x, g):
    return x * lax.rsqrt(jnp.mean(x * x, axis=-1, keepdims=True) + EPS) * g


def _dot(a, b):
    return jnp.dot(a, b, preferred_element_type=F32)


def _dot_nt(a, b):
    return lax.dot_general(a, b, (((1,), (1,)), ((), ())), preferred_element_type=F32)


def _params(sem, limit=VMEM_LIMIT):
    return pltpu.CompilerParams(dimension_semantics=sem, vmem_limit_bytes=limit)


def _rope_kernel(pos_ref, invf_ref, cc_ref, ss_ref):
    ang = pos_ref[...].astype(F32) * invf_ref[...]
    c = jnp.cos(ang)
    s = jnp.sin(ang)
    lane = lax.broadcasted_iota(jnp.int32, ang.shape, 1)
    half = D_ROPE // 2
    cc_ref[...] = jnp.where(lane < D_QK, c, 0.0)
    ss_ref[...] = jnp.where((lane >= D_NOPE) & (lane < D_NOPE + half), -s,
                            jnp.where((lane >= D_NOPE + half) & (lane < D_QK), s, 0.0))


def _rope_tables(pos_col, invf):
    t = pos_col.shape[0]
    out = jax.ShapeDtypeStruct((t, HEAD_PAD), F32)
    spec = pl.BlockSpec((ROPE_ROWS, HEAD_PAD), lambda i: (i, 0))
    return pl.pallas_call(
        _rope_kernel,
        grid=(t // ROPE_ROWS,),
        in_specs=[pl.BlockSpec((ROPE_ROWS, 1), lambda i: (i, 0)), _const_spec((1, HEAD_PAD))],
        out_specs=[spec, spec],
        out_shape=[out, out],
        compiler_params=_params(("arbitrary",)),
        name="rope_tables",
    )(pos_col, invf)


def _memkv_kernel(mem_ref, g_ref, w_ref, gk_ref, k_ref, v_ref):
    m = _rms(mem_ref[0], g_ref[...]).astype(BF16)
    kv = _dot(m, w_ref[...])
    for h in range(X_HEADS):
        sl = slice(h * X_HEAD_DIM, (h + 1) * X_HEAD_DIM)
        k_ref[0, :, sl] = _rms(kv[:, sl], gk_ref[...]).astype(BF16)
    v_ref[0] = kv[:, X_WIDTH:].astype(BF16)


def _memkv(mem, g, w, gk):
    b = mem.shape[0]
    out = jax.ShapeDtypeStruct((b, MEM_LEN, X_WIDTH), BF16)
    spec = pl.BlockSpec((1, MEM_LEN, X_WIDTH), lambda i: (i, 0, 0))
    return pl.pallas_call(
        _memkv_kernel,
        grid=(b,),
        in_specs=[pl.BlockSpec((1, MEM_LEN, D_MODEL), lambda i: (i, 0, 0)),
                  _const_spec(g.shape), _const_spec(w.shape), _const_spec(gk.shape)],
        out_specs=[spec, spec],
        out_shape=[out, out],
        compiler_params=_params(("arbitrary",)),
        name="mem_kv",
    )(mem, g, w, gk)


def _front_kernel(x_ref, cc_ref, ss_ref, kx_ref, vx_ref, gmix_ref, win_ref,
                  gqa_ref, wq_ref, gkva_ref, wk_ref, wvt_ref, gq_ref, gqs_ref, gk_ref, gks_ref,
                  gxq_ref, q_ref, k_ref, vt_ref, u_ref, yc_ref):
    tm = x_ref.shape[0]
    h = _rms(x_ref[...], gmix_ref[...]).astype(BF16)
    p = _dot(h, win_ref[...])
    o_ckv = Q_LORA
    o_kr = o_ckv + KV_LORA
    o_krs = o_kr + HEAD_PAD
    o_u = o_krs + HEAD_PAD
    o_xq = o_u + SSM_WIDTH
    cq = p[:, :o_ckv]
    ckv = p[:, o_ckv:o_kr]
    kr = p[:, o_kr:o_krs]
    krs = p[:, o_krs:o_u]
    u_ref[...] = p[:, o_u:o_xq].astype(BF16)
    xq = p[:, o_xq:]

    cqn = _rms(cq, gqa_ref[...]).astype(BF16)
    ckvn = _rms(ckv, gkva_ref[...]).astype(BF16)

    vt = _dot_nt(wvt_ref[...], ckvn)
    ones = jnp.ones((V_ROWS - D_V, TK), BF16)
    for hd in range(MLA_HEADS):
        for c in range(tm // TK):
            vt_ref[0, hd, c, :D_V, :] = vt[hd * D_V:(hd + 1) * D_V, c * TK:(c + 1) * TK].astype(BF16)
            vt_ref[0, hd, c, D_V:, :] = ones

    cc = cc_ref[...]
    ss = ss_ref[...]
    nq = MLA_HEADS * HEAD_PAD

    def inv_rms(a):
        return lax.rsqrt(jnp.sum(a * a, axis=-1, keepdims=True) * (1.0 / D_QK) + EPS)

    qa = _dot(cqn, wq_ref[...])
    gc = gq_ref[...] * cc
    gs = gqs_ref[...] * ss
    scale = D_QK ** -0.5 * math.log2(math.e)
    for hd in range(MLA_HEADS):
        a = qa[:, hd * HEAD_PAD:(hd + 1) * HEAD_PAD]
        a_s = qa[:, nq + hd * HEAD_PAD:nq + (hd + 1) * HEAD_PAD]
        q_ref[0, hd] = ((inv_rms(a) * scale) * (a * gc + a_s * gs)).astype(BF16)

    ka = _dot(ckvn, wk_ref[...])
    gc = gk_ref[...] * cc
    rot = krs * (gks_ref[...] * ss)
    for hd in range(MLA_HEADS):
        a = ka[:, hd * HEAD_PAD:(hd + 1) * HEAD_PAD] + kr
        k_ref[0, hd] = (inv_rms(a) * (a * gc + rot)).astype(BF16)

    xscale = X_HEAD_DIM ** -0.5
    for hd in range(X_HEADS):
        sl = slice(hd * X_HEAD_DIM, (hd + 1) * X_HEAD_DIM)
        qx = (_rms(xq[:, sl], gxq_ref[...]) * xscale).astype(BF16)
        s = _dot_nt(qx, kx_ref[0, :, sl])
        pm = jnp.exp(s - jnp.max(s, axis=-1, keepdims=True))
        l = jnp.sum(pm, axis=-1, keepdims=True)
        yc_ref[:, sl] = (_dot(pm.astype(BF16), vx_ref[0, :, sl]) / l).astype(BF16)


def _front(x2, tables, kx, vx, w, bsz, seq):
    t = bsz * seq
    tm = TM_FRONT
    nb = seq // tm
    row = lambda c: pl.BlockSpec((tm, c), lambda i: (i, 0))
    hspec = pl.BlockSpec((1, MLA_HEADS, tm, HEAD_PAD), lambda i: (i // nb, 0, i % nb, 0))
    mspec = pl.BlockSpec((1, MEM_LEN, X_WIDTH), lambda i: (i // nb, 0, 0))
    consts = [w["gmix"], w["win"], w["gqa"], w["wq"], w["gkva"], w["wk"], w["wvt"],
              w["gq"], w["gqs"], w["gk"], w["gks"], w["gxq"]]
    hshape = jax.ShapeDtypeStruct((bsz, MLA_HEADS, seq, HEAD_PAD), BF16)
    tshape = jax.ShapeDtypeStruct((t, SSM_WIDTH), BF16)
    vtshape = jax.ShapeDtypeStruct((bsz, MLA_HEADS, seq // TK, V_ROWS, TK), BF16)
    vtspec = pl.BlockSpec((1, MLA_HEADS, tm // TK, V_ROWS, TK), lambda i: (i // nb, 0, i % nb, 0, 0))
    return pl.pallas_call(
        _front_kernel,
        grid=(t // tm,),
        in_specs=[row(D_MODEL), row(HEAD_PAD), row(HEAD_PAD), mspec, mspec]
                 + [_const_spec(c.shape) for c in consts],
        out_specs=[hspec, hspec, vtspec, row(SSM_WIDTH), row(X_WIDTH)],
        out_shape=[hshape, hshape, vtshape, tshape, tshape],
        compiler_params=_params(("arbitrary",)),
        name="front",
    )(x2, *tables, kx, vx, *consts)


def _attn_kernel(q_ref, k_ref, vt_ref, o_ref, st_scr, m_scr, acc_scr):
    qi = pl.program_id(2)
    m_scr[...] = jnp.full(m_scr.shape, -jnp.inf, F32)
    acc_scr[...] = jnp.zeros(acc_scr.shape, F32)

    def scores(kb, slot):
        r = pl.multiple_of(kb * TK, TK)
        for hh in range(ATTN_HEADS):
            st_scr[slot, hh] = _dot_nt(k_ref[0, hh, pl.ds(r, TK), :], q_ref[0, hh])

    def consume(kb, slot, masked):
        for hh in range(ATTN_HEADS):
            st = st_scr[slot, hh]
            if masked:
                key = kb * TK + lax.broadcasted_iota(jnp.int32, st.shape, 0)
                qry = qi * TQ + lax.broadcasted_iota(jnp.int32, st.shape, 1)
                st = jnp.where(key <= qry, st, -jnp.inf)
            m_old = m_scr[hh]
            m_new = jnp.maximum(m_old, jnp.max(st, axis=0, keepdims=True))
            pm = jnp.exp2(st - m_new).astype(BF16)
            alpha = jnp.exp2(m_old - m_new)
            acc_scr[hh] = alpha * acc_scr[hh] + _dot(vt_ref[0, hh, kb], pm)
            m_scr[hh] = m_new

    steps = TQ // TK
    scores(0, 0)

    def body(j, c):
        for s in range(steps):
            kb = j * steps + s
            scores(kb + 1, (s + 1) % 2)
            consume(kb, s % 2, False)
        return c

    lax.fori_loop(0, qi, body, 0)
    for s in range(steps):
        kb = qi * steps + s
        if s + 1 < steps:
            scores(kb + 1, (s + 1) % 2)
        consume(kb, s % 2, True)
    outs = []
    for hh in range(ATTN_HEADS):
        a = acc_scr[hh]
        outs.append(a[:D_V] / a[D_V:D_V + 1])
    o_ref[0] = jnp.concatenate(outs, axis=0).T.astype(BF16)


def _attention(q, k, vt):
    bsz, nh, seq, _ = q.shape
    assert TQ % TK == 0 and (TQ // TK) % 2 == 0
    nh_blk = ATTN_HEADS
    return pl.pallas_call(
        _attn_kernel,
        grid=(bsz, nh // nh_blk, seq // TQ),
        in_specs=[pl.BlockSpec((1, nh_blk, TQ, HEAD_PAD), lambda b, hg, i: (b, hg, i, 0)),
                  pl.BlockSpec((1, nh_blk, seq, HEAD_PAD), lambda b, hg, i: (b, hg, 0, 0)),
                  pl.BlockSpec((1, nh_blk, seq // TK, V_ROWS, TK), lambda b, hg, i: (b, hg, 0, 0, 0))],
        out_specs=pl.BlockSpec((1, TQ, nh_blk * D_V), lambda b, hg, i: (b, i, hg)),
        out_shape=jax.ShapeDtypeStruct((bsz, seq, nh * D_V), BF16),
        scratch_shapes=[pltpu.VMEM((2, nh_blk, TK, TQ), F32), pltpu.VMEM((nh_blk, 1, TQ), F32),
                        pltpu.VMEM((nh_blk, V_ROWS, TQ), F32)],
        compiler_params=_params(("arbitrary", "arbitrary", "arbitrary")),
        name="mla_attention",
    )(q, k, vt)


def _s5_prep_kernel(lr_ref, li_ref, ldt_ref, lrc_ref, lic_ref, ldtc_ref, bre_ref, bim_ref,
                    a_ref, ptab_ref, bbre_ref, bbim_ref):
    ns = SSM_NS
    dt = jnp.exp(ldt_ref[0])
    lr = lr_ref[0] * dt
    li = li_ref[0] * dt
    k = (lax.broadcasted_iota(jnp.int32, (S5_SEG, 1), 0) + 1).astype(F32)
    mag = jnp.exp(k * lr)
    ang = k * li
    ptab_ref[0, :, :ns] = mag * jnp.cos(ang)
    ptab_ref[0, :, ns:] = mag * jnp.sin(ang)
    mag1 = jnp.exp(lr)
    a_ref[0, :, :ns] = mag1 * jnp.cos(li)
    a_ref[0, :, ns:] = mag1 * jnp.sin(li)
    dtc = jnp.exp(ldtc_ref[0])
    lrc = lrc_ref[0]
    lic = lic_ref[0]
    magc = jnp.exp(lrc * dtc)
    e_re = magc * jnp.cos(lic * dtc) - 1.0
    e_im = magc * jnp.sin(lic * dtc)
    den = lrc * lrc + lic * lic
    f_re = (e_re * lrc + e_im * lic) / den
    f_im = (e_im * lrc - e_re * lic) / den
    bre = bre_ref[0]
    bim = bim_ref[0]
    bbre_ref[0] = f_re * bre - f_im * bim
    bbim_ref[0] = f_re * bim + f_im * bre


def _s5_prep(lam_re, lam_im, log_dt, b_re, b_im):
    depth = lam_re.shape[0]
    ns = SSM_NS
    ldt = jnp.broadcast_to(log_dt[:, :, None], lam_re.shape)
    rowv = lambda a: a.reshape(depth, 1, ns)
    colv = lambda a: a.reshape(depth, ns, 1)
    bcol = lambda a: a.reshape(depth, ns, SSM_GROUP_CH)
    spec = lambda s: pl.BlockSpec((1,) + s, lambda i: (i, 0, 0))
    return pl.pallas_call(
        _s5_prep_kernel,
        grid=(depth,),
        in_specs=[spec((1, ns))] * 3 + [spec((ns, 1))] * 3 + [spec((ns, SSM_GROUP_CH))] * 2,
        out_specs=[spec((1, 2 * ns)), spec((S5_SEG, 2 * ns)),
                   spec((ns, SSM_GROUP_CH)), spec((ns, SSM_GROUP_CH))],
        out_shape=[jax.ShapeDtypeStruct((depth, 1, 2 * ns), F32),
                   jax.ShapeDtypeStruct((depth, S5_SEG, 2 * ns), F32),
                   jax.ShapeDtypeStruct((depth, ns, SSM_GROUP_CH), F32),
                   jax.ShapeDtypeStruct((depth, ns, SSM_GROUP_CH), F32)],
        compiler_params=_params(("arbitrary",)),
        name="s5_prep",
    )(rowv(lam_re), rowv(lam_im), rowv(ldt), colv(lam_re), colv(lam_im), colv(ldt),
      bcol(b_re), bcol(b_im))


def _s5_kernel(u_ref, bmat_ref, cre_ref, cim_ref, a_ref, ptab_ref, d_ref, wglu_ref, bglu_ref,
               o_ref, bu_scr, carry_scr, c_scr):
    ns = SSM_NS
    cw = S5_COLS

    @pl.when(pl.program_id(1) == 0)
    def _():
        carry_scr[...] = jnp.zeros(carry_scr.shape, F32)

    u = u_ref[0]
    bu_scr[...] = _dot(u, bmat_ref[...])

    for cg in range(ns // cw):
        re = slice(cg * cw, (cg + 1) * cw)
        im = slice(ns + cg * cw, ns + (cg + 1) * cw)
        a_re = jnp.broadcast_to(a_ref[:, re], (SUBLANES, cw))
        a_im = jnp.broadcast_to(a_ref[:, im], (SUBLANES, cw))

        def scan(i, z):
            z_re, z_im = z
            rows = pl.ds(pl.multiple_of(i * SUBLANES, SUBLANES), SUBLANES)
            n_re = a_re * z_re - a_im * z_im + bu_scr[rows, re]
            n_im = a_re * z_im + a_im * z_re + bu_scr[rows, im]
            bu_scr[rows, re] = n_re
            bu_scr[rows, im] = n_im
            return n_re, n_im

        zero = jnp.zeros((SUBLANES, cw), F32)
        e_re, e_im = lax.fori_loop(0, S5_SEG, scan, (zero, zero), unroll=4)

        p_re = ptab_ref[S5_SEG - 1:S5_SEG, re]
        p_im = ptab_ref[S5_SEG - 1:S5_SEG, im]
        cur_re = carry_scr[:, re]
        cur_im = carry_scr[:, im]
        for j in range(SUBLANES):
            c_scr[j:j + 1, re] = cur_re
            c_scr[j:j + 1, im] = cur_im
            n_re = e_re[j:j + 1, :] + p_re * cur_re - p_im * cur_im
            n_im = e_im[j:j + 1, :] + p_re * cur_im + p_im * cur_re
            cur_re, cur_im = n_re, n_im
        carry_scr[:, re] = cur_re
        carry_scr[:, im] = cur_im
        c_re = c_scr[:, re]
        c_im = c_scr[:, im]

        def fix(i, carry):
            rows = pl.ds(pl.multiple_of(i * SUBLANES, SUBLANES), SUBLANES)
            q_re = ptab_ref[pl.ds(i, 1), re]
            q_im = ptab_ref[pl.ds(i, 1), im]
            bu_scr[rows, re] = bu_scr[rows, re] + q_re * c_re - q_im * c_im
            bu_scr[rows, im] = bu_scr[rows, im] + q_re * c_im + q_im * c_re
            return carry

        lax.fori_loop(0, S5_SEG, fix, 0, unroll=4)

    y = (_dot(bu_scr[:, :ns].astype(BF16), cre_ref[...])
         - _dot(bu_scr[:, ns:].astype(BF16), cim_ref[...])
         + d_ref[...] * u.astype(F32))
    y = jax.nn.gelu(y)
    z = _dot(y.astype(BF16), wglu_ref[...]) + bglu_ref[...]
    o_ref[0] = (y * jax.nn.sigmoid(z)).astype(BF16)


def _s5(u_perm, bmat, cre, cim, a, ptab, d, wglu, bglu):
    bsz, seq, _ = u_perm.shape
    ns = SSM_NS
    blk = pl.BlockSpec((1, S5_CHUNK, SSM_WIDTH), lambda b, j: (b, j, 0))
    consts = [bmat, cre, cim, a, ptab, d, wglu, bglu]
    return pl.pallas_call(
        _s5_kernel,
        grid=(bsz, seq // S5_CHUNK),
        in_specs=[blk] + [_const_spec(c.shape) for c in consts],
        out_specs=blk,
        out_shape=jax.ShapeDtypeStruct((bsz, seq, SSM_WIDTH), BF16),
        scratch_shapes=[pltpu.VMEM((S5_CHUNK, 2 * ns), F32), pltpu.VMEM((1, 2 * ns), F32),
                        pltpu.VMEM((SUBLANES, 2 * ns), F32)],
        compiler_params=_params(("arbitrary", "arbitrary")),
        name="s5_scan",
    )(u_perm, *consts)


def _merge_kernel(x_ref, ya_ref, yb_ref, yc_ref, gmix_ref, wg_ref, bg_ref, woa_ref, wob_ref,
                  woc_ref, wout_ref, o_ref):
    x = x_ref[...]
    h = _rms(x, gmix_ref[...]).astype(BF16)
    merged = None
    for br, (y_ref, wo_ref) in enumerate(((ya_ref, woa_ref), (yb_ref, wob_ref), (yc_ref, woc_ref))):
        sl = slice(br * D_MODEL, (br + 1) * D_MODEL)
        gate = jax.nn.sigmoid(_dot(h, wg_ref[:, sl]) + bg_ref[:, sl])
        term = gate * _dot(y_ref[...], wo_ref[...])
        merged = term if merged is None else merged + term
    o_ref[...] = x + _dot(merged.astype(BF16), wout_ref[...])


def _merge(x2, ya, yb, yc, w):
    t = x2.shape[0]
    tm = TM_MERGE
    row = lambda c: pl.BlockSpec((tm, c), lambda i: (i, 0))
    consts = [w["gmix"], w["wg"], w["bg"], w["woa"], w["wob"], w["woc"], w["wout"]]
    return pl.pallas_call(
        _merge_kernel,
        grid=(t // tm,),
        in_specs=[row(D_MODEL), row(MLA_WIDTH), row(SSM_WIDTH), row(X_WIDTH)]
                 + [_const_spec(c.shape) for c in consts],
        out_specs=row(D_MODEL),
        out_shape=jax.ShapeDtypeStruct((t, D_MODEL), F32),
        compiler_params=_params(("arbitrary",)),
        name="merge",
    )(x2, ya, yb, yc, *consts)


def _ffn_kernel(x_ref, g_ref, wug_ref, wuv_ref, cwg_ref, cwv_ref, cbg_ref, cbv_ref, wd_ref,
                o_ref, carry_g, carry_v, *, blocks_per_seq):
    tm = x_ref.shape[0]

    @pl.when(pl.program_id(0) % blocks_per_seq == 0)
    def _():
        carry_g[...] = jnp.zeros(carry_g.shape, F32)
        carry_v[...] = jnp.zeros(carry_v.shape, F32)

    x = x_ref[...]
    h2 = _rms(x, g_ref[...]).astype(BF16)
    row = lax.broadcasted_iota(jnp.int32, (tm, 1), 0)

    def conv(cs, w_ref, cw_ref, cb_ref, carry):
        up = _dot(h2, w_ref[:, cs])
        prev = carry[:, cs]
        carry[:, cs] = up[tm - SUBLANES:, :]
        p1 = prev[SUBLANES - 1:SUBLANES, :]
        p2 = prev[SUBLANES - 2:SUBLANES - 1, :]
        m1 = jnp.where(row == 0, p1, pltpu.roll(up, 1, 0))
        m2 = jnp.where(row == 0, p2, jnp.where(row == 1, p1, pltpu.roll(up, 2, 0)))
        return cw_ref[0:1, cs] * m2 + cw_ref[1:2, cs] * m1 + cw_ref[2:3, cs] * up + cb_ref[:, cs]

    acc = jnp.zeros((tm, D_MODEL), F32)
    for c in range(D_FF // FFN_CHUNK):
        cs = slice(c * FFN_CHUNK, (c + 1) * FFN_CHUNK)
        gv = conv(cs, wug_ref, cwg_ref, cbg_ref, carry_g)
        vv = conv(cs, wuv_ref, cwv_ref, cbv_ref, carry_v)
        act = (gv * jax.nn.sigmoid(gv) * vv).astype(BF16)
        acc = acc + _dot(act, wd_ref[cs, :])
    o_ref[...] = x + acc


def _ffn(x2, w, seq):
    t = x2.shape[0]
    tm = TM_FFN
    row = pl.BlockSpec((tm, D_MODEL), lambda i: (i, 0))
    consts = [w["gffn"], w["wug"], w["wuv"], w["cwg"], w["cwv"], w["cbg"], w["cbv"], w["wd"]]
    return pl.pallas_call(
        functools.partial(_ffn_kernel, blocks_per_seq=seq // tm),
        grid=(t // tm,),
        in_specs=[row] + [_const_spec(c.shape) for c in consts],
        out_specs=row,
        out_shape=jax.ShapeDtypeStruct((t, D_MODEL), F32),
        scratch_shapes=[pltpu.VMEM((SUBLANES, D_FF), F32), pltpu.VMEM((SUBLANES, D_FF), F32)],
        compiler_params=_params(("arbitrary",)),
        name="ffn",
    )(x2, *consts)


def _block_diag(blocks):
    g, r, c = blocks.shape
    eye = jnp.eye(g, dtype=bool)
    full = jnp.where(eye[:, None, :, None], blocks[:, :, None, :], jnp.zeros((), blocks.dtype))
    return full.reshape(g * r, g * c)


def _pad_lanes(a, lo, width=HEAD_PAD):
    pad = [(0, 0)] * (a.ndim - 1) + [(lo, width - lo - a.shape[-1])]
    return jnp.pad(a, pad)


def _swap_rope(a):
    half = D_ROPE // 2
    lo = a[..., D_NOPE:D_NOPE + half]
    hi = a[..., D_NOPE + half:D_QK]
    return jnp.concatenate([jnp.zeros_like(a[..., :D_NOPE]), hi, lo,
                            jnp.zeros_like(a[..., D_QK:])], axis=-1)


def _layer_weights(i, p):
    w_in = p["w_in"][i]
    o = 0
    parts = {}
    for name, width in (("cq", Q_LORA), ("ckv", KV_LORA), ("kr", D_ROPE), ("u", SSM_WIDTH),
                        ("xq", X_WIDTH), ("g", N_BRANCH * D_MODEL)):
        parts[name] = w_in[:, o:o + width]
        o += width
    kr = _pad_lanes(parts["kr"], D_NOPE)
    win = jnp.concatenate([parts["cq"], parts["ckv"], kr, _swap_rope(kr),
                           parts["u"], parts["xq"]], axis=1).astype(BF16)
    wq = _pad_lanes(p["w_q_b"][i].reshape(Q_LORA, MLA_HEADS, D_QK), 0)
    wq = jnp.concatenate([wq.reshape(Q_LORA, -1), _swap_rope(wq).reshape(Q_LORA, -1)], axis=1)
    wkv = p["w_kv_b"][i].reshape(KV_LORA, MLA_HEADS, D_NOPE + D_V)
    wk = _pad_lanes(wkv[:, :, :D_NOPE], 0).reshape(KV_LORA, -1)
    wvt = wkv[:, :, D_NOPE:].reshape(KV_LORA, MLA_WIDTH).T
    row = lambda a: a.reshape(1, -1)
    gq = _pad_lanes(row(p["q_norm_g"][i]), 0)
    gk = _pad_lanes(row(p["k_norm_g"][i]), 0)
    w_up = p["w_up"][i]
    conv_w = p["conv_w"][i]
    conv_b = p["conv_b"][i]
    return {
        "gmix": row(p["norm_mix_g"][i]),
        "win": win,
        "gqa": row(p["q_a_norm_g"][i]),
        "wq": wq.astype(BF16),
        "gkva": row(p["kv_a_norm_g"][i]),
        "wk": wk.astype(BF16),
        "wvt": wvt.astype(BF16),
        "gq": gq,
        "gqs": _swap_rope(gq),
        "gk": gk,
        "gks": _swap_rope(gk),
        "gxq": row(p["xq_norm_g"][i]),
        "wg": parts["g"].astype(BF16),
        "bg": row(p["b_gate"][i]),
        "woa": p["w_o_mla"][i].astype(BF16),
        "wob": p["w_o_ssm"][i].astype(BF16),
        "woc": p["w_o_cross"][i].astype(BF16),
        "wout": p["w_out"][i].astype(BF16),
        "gffn": row(p["norm_ffn_g"][i]),
        "wug": w_up[:, :D_FF].astype(BF16),
        "wuv": w_up[:, D_FF:].astype(BF16),
        "cwg": conv_w[:, :D_FF],
        "cwv": conv_w[:, D_FF:],
        "cbg": row(conv_b[:D_FF]),
        "cbv": row(conv_b[D_FF:]),
        "wd": p["w_down"][i].astype(BF16),
    }


def _segment_major(a, bsz, seq):
    c = a.shape[-1]
    a = a.reshape(bsz, seq // S5_CHUNK, SUBLANES, S5_SEG, c)
    return a.transpose(0, 1, 3, 2, 4).reshape(bsz, seq, c)


def _time_major(a, bsz, seq):
    c = a.shape[-1]
    a = a.reshape(bsz, seq // S5_CHUNK, S5_SEG, SUBLANES, c)
    return a.transpose(0, 1, 3, 2, 4).reshape(bsz, seq, c)


def kernel(x, mem, positions, norm_mix_g, w_in, q_a_norm_g, w_q_b, kv_a_norm_g, w_kv_b, q_norm_g, k_norm_g, w_o_mla, ssm_lambda_re, ssm_lambda_im, ssm_log_dt, ssm_b_re, ssm_b_im, ssm_c_re, ssm_c_im, ssm_d, w_glu, b_glu, w_o_ssm, mem_norm_g, w_mem_kv, xq_norm_g, xk_norm_g, w_o_cross, b_gate, w_out, norm_ffn_g, w_up, conv_w, conv_b, w_down):
    p = dict(norm_mix_g=norm_mix_g, w_in=w_in, q_a_norm_g=q_a_norm_g, w_q_b=w_q_b,
             kv_a_norm_g=kv_a_norm_g, w_kv_b=w_kv_b, q_norm_g=q_norm_g, k_norm_g=k_norm_g,
             w_o_mla=w_o_mla, w_o_ssm=w_o_ssm, w_o_cross=w_o_cross, b_gate=b_gate, w_out=w_out,
             norm_ffn_g=norm_ffn_g, w_up=w_up, conv_w=conv_w, conv_b=conv_b, w_down=w_down,
             xq_norm_g=xq_norm_g)
    bsz, seq, _ = x.shape
    t = bsz * seq
    assert seq % max(TM_FRONT, TM_MERGE, TM_FFN, TQ, S5_CHUNK) == 0 and t % ROPE_ROWS == 0

    inv_freq = ROPE_THETA ** (-jnp.arange(0, D_ROPE, 2, dtype=F32) / D_ROPE)
    invf = jnp.concatenate([jnp.zeros((D_NOPE,), F32), inv_freq, inv_freq,
                            jnp.zeros((HEAD_PAD - D_QK,), F32)]).reshape(1, HEAD_PAD)
    tables = _rope_tables(positions.reshape(t, 1), invf)

    a_all, ptab_all, bbre_all, bbim_all = _s5_prep(ssm_lambda_re, ssm_lambda_im, ssm_log_dt,
                                                   ssm_b_re, ssm_b_im)

    x2 = x.reshape(t, D_MODEL)
    for i in range(DEPTH):
        w = _layer_weights(i, p)
        kx, vx = _memkv(mem, mem_norm_g[i].reshape(1, -1), w_mem_kv[i].astype(BF16),
                        xk_norm_g[i].reshape(1, -1))
        q, k, vt, u, yc = _front(x2, tables, kx, vx, w, bsz, seq)
        ya = _attention(q, k, vt).reshape(t, MLA_WIDTH)

        shp = (SSM_GROUPS, SSM_STATE, SSM_GROUP_CH)
        bmat = jnp.concatenate(
            [_block_diag(bbre_all[i].reshape(shp).transpose(0, 2, 1)),
             _block_diag(bbim_all[i].reshape(shp).transpose(0, 2, 1))], axis=1).astype(BF16)
        cre = _block_diag(ssm_c_re[i].transpose(0, 2, 1)).astype(BF16)
        cim = _block_diag(ssm_c_im[i].transpose(0, 2, 1)).astype(BF16)
        yb = _s5(_segment_major(u.reshape(bsz, seq, SSM_WIDTH), bsz, seq), bmat, cre, cim,
                 a_all[i], ptab_all[i], ssm_d[i].reshape(1, SSM_WIDTH), w_glu[i].astype(BF16),
                 b_glu[i].reshape(1, SSM_WIDTH))
        yb = _time_major(yb, bsz, seq).reshape(t, SSM_WIDTH)

        x2 = _merge(x2, ya, yb, yc, w)
        x2 = _ffn(x2, w, seq)
    return x2.reshape(bsz, seq, D_MODEL)
```

```python
import functools
import math

import jax
import jax.numpy as jnp
from jax import lax
from jax.experimental import pallas as pl
from jax.experimental.pallas import tpu as pltpu

F32 = jnp.float32
BF16 = jnp.bfloat16

D_MODEL = 1024
DEPTH = 2
MEM_LEN = 256
EPS = 1e-6
MLA_HEADS = 8
Q_LORA = 384
KV_LORA = 256
D_NOPE = 64
D_ROPE = 32
D_QK = D_NOPE + D_ROPE
D_V = 64
MLA_WIDTH = MLA_HEADS * D_V
ROPE_THETA = 10000.0
SSM_GROUPS = 32
SSM_GROUP_CH = 16
SSM_WIDTH = SSM_GROUPS * SSM_GROUP_CH
SSM_STATE = 64
SSM_NS = SSM_GROUPS * SSM_STATE
X_HEADS = 4
X_HEAD_DIM = 128
X_WIDTH = X_HEADS * X_HEAD_DIM
N_BRANCH = 3
D_FF = 2816
CONV_WIDTH = 3

LANES = 128
SUBLANES = 8
HEAD_PAD = LANES

TM_FRONT = 512
TM_MERGE = 512
TM_FFN = 512
FFN_CHUNK = 256
TQ = 512
TK = 256
ATTN_HEADS = 4
V_ROWS = D_V + 16
S5_SEG = 32
S5_CHUNK = SUBLANES * S5_SEG
S5_COLS = 512
S5_HALVES = 2
ROPE_ROWS = 1024

VMEM_LIMIT = 56 * 1024 * 1024


def _const_spec(shape):
    nd = len(shape)
    return pl.BlockSpec(shape, lambda *_: (0,) * nd, pipeline_mode=pl.Buffered(1))


def _rms(x, g):
    return x * lax.rsqrt(jnp.mean(x * x, axis=-1, keepdims=True) + EPS) * g


def _dot(a, b):
    return jnp.dot(a, b, preferred_element_type=F32)


def _dot_nt(a, b):
    return lax.dot_general(a, b, (((1,), (1,)), ((), ())), preferred_element_type=F32)


def _params(sem, limit=VMEM_LIMIT):
    return pltpu.CompilerParams(dimension_semantics=sem, vmem_limit_bytes=limit)


def _rope_kernel(pos_ref, invf_ref, cc_ref, ss_ref):
    ang = pos_ref[...].astype(F32) * invf_ref[...]
    c = jnp.cos(ang)
    s = jnp.sin(ang)
    lane = lax.broadcasted_iota(jnp.int32, ang.shape, 1)
    half = D_ROPE // 2
    cc_ref[...] = jnp.where(lane < D_QK, c, 0.0)
    ss_ref[...] = jnp.where((lane >= D_NOPE) & (lane < D_NOPE + half), -s,
                            jnp.where((lane >= D_NOPE + half) & (lane < D_QK), s, 0.0))


def _rope_tables(pos_col, invf):
    t = pos_col.shape[0]
    out = jax.ShapeDtypeStruct((t, HEAD_PAD), F32)
    spec = pl.BlockSpec((ROPE_ROWS, HEAD_PAD), lambda i: (i, 0))
    return pl.pallas_call(
        _rope_kernel,
        grid=(t // ROPE_ROWS,),
        in_specs=[pl.BlockSpec((ROPE_ROWS, 1), lambda i: (i, 0)), _const_spec((1, HEAD_PAD))],
        out_specs=[spec, spec],
        out_shape=[out, out],
        compiler_params=_params(("arbitrary",)),
        name="rope_tables",
    )(pos_col, invf)


def _memkv_kernel(mem_ref, g_ref, w_ref, gk_ref, k_ref, v_ref):
    m = _rms(mem_ref[0], g_ref[...]).astype(BF16)
    kv = _dot(m, w_ref[...])
    for h in range(X_HEADS):
        sl = slice(h * X_HEAD_DIM, (h + 1) * X_HEAD_DIM)
        k_ref[0, :, sl] = _rms(kv[:, sl], gk_ref[...]).astype(BF16)
    v_ref[0] = kv[:, X_WIDTH:].astype(BF16)


def _memkv(mem, g, w, gk):
    b = mem.shape[0]
    out = jax.ShapeDtypeStruct((b, MEM_LEN, X_WIDTH), BF16)
    spec = pl.BlockSpec((1, MEM_LEN, X_WIDTH), lambda i: (i, 0, 0))
    return pl.pallas_call(
        _memkv_kernel,
        grid=(b,),
        in_specs=[pl.BlockSpec((1, MEM_LEN, D_MODEL), lambda i: (i, 0, 0)),
                  _const_spec(g.shape), _const_spec(w.shape), _const_spec(gk.shape)],
        out_specs=[spec, spec],
        out_shape=[out, out],
        compiler_params=_params(("arbitrary",)),
        name="mem_kv",
    )(mem, g, w, gk)


def _front_kernel(x_ref, cc_ref, ss_ref, kx_ref, vx_ref, gmix_ref, win_ref,
                  gqa_ref, wq_ref, gkva_ref, wk_ref, wvt_ref, gq_ref, gqs_ref, gk_ref, gks_ref,
                  gxq_ref, q_ref, k_ref, vt_ref, u_ref, yc_ref):
    tm = x_ref.shape[0]
    h = _rms(x_ref[...], gmix_ref[...]).astype(BF16)
    p = _dot(h, win_ref[...])
    o_ckv = Q_LORA
    o_kr = o_ckv + KV_LORA
    o_krs = o_kr + HEAD_PAD
    o_u = o_krs + HEAD_PAD
    o_xq = o_u + SSM_WIDTH
    cq = p[:, :o_ckv]
    ckv = p[:, o_ckv:o_kr]
    kr = p[:, o_kr:o_krs]
    krs = p[:, o_krs:o_u]
    u_ref[...] = p[:, o_u:o_xq].astype(BF16)
    xq = p[:, o_xq:]

    cqn = _rms(cq, gqa_ref[...]).astype(BF16)
    ckvn = _rms(ckv, gkva_ref[...]).astype(BF16)

    vt = _dot_nt(wvt_ref[...], ckvn)
    ones = jnp.ones((V_ROWS - D_V, TK), BF16)
    for hd in range(MLA_HEADS):
        for c in range(tm // TK):
            vt_ref[0, hd, c, :D_V, :] = vt[hd * D_V:(hd + 1) * D_V, c * TK:(c + 1) * TK].astype(BF16)
            vt_ref[0, hd, c, D_V:, :] = ones

    cc = cc_ref[...]
    ss = ss_ref[...]
    nq = MLA_HEADS * HEAD_PAD

    def inv_rms(a):
        return lax.rsqrt(jnp.sum(a * a, axis=-1, keepdims=True) * (1.0 / D_QK) + EPS)

    qa = _dot(cqn, wq_ref[...])
    gc = gq_ref[...] * cc
    gs = gqs_ref[...] * ss
    scale = D_QK ** -0.5 * math.log2(math.e)
    for hd in range(MLA_HEADS):
        a = qa[:, hd * HEAD_PAD:(hd + 1) * HEAD_PAD]
        a_s = qa[:, nq + hd * HEAD_PAD:nq + (hd + 1) * HEAD_PAD]
        q_ref[0, hd] = ((inv_rms(a) * scale) * (a * gc + a_s * gs)).astype(BF16)

    ka = _dot(ckvn, wk_ref[...])
    gc = gk_ref[...] * cc
    rot = krs * (gks_ref[...] * ss)
    for hd in range(MLA_HEADS):
        a = ka[:, hd * HEAD_PAD:(hd + 1) * HEAD_PAD] + kr
        k_ref[0, hd] = (inv_rms(a) * (a * gc + rot)).astype(BF16)

    xscale = X_HEAD_DIM ** -0.5
    for hd in range(X_HEADS):
        sl = slice(hd * X_HEAD_DIM, (hd + 1) * X_HEAD_DIM)
        qx = (_rms(xq[:, sl], gxq_ref[...]) * xscale).astype(BF16)
        s = _dot_nt(qx, kx_ref[0, :, sl])
        pm = jnp.exp(s - jnp.max(s, axis=-1, keepdims=True))
        l = jnp.sum(pm, axis=-1, keepdims=True)
        yc_ref[:, sl] = (_dot(pm.astype(BF16), vx_ref[0, :, sl]) / l).astype(BF16)


def _front(x2, tables, kx, vx, w, bsz, seq):
    t = bsz * seq
    tm = TM_FRONT
    nb = seq // tm
    row = lambda c: pl.BlockSpec((tm, c), lambda i: (i, 0))
    hspec = pl.BlockSpec((1, MLA_HEADS, tm, HEAD_PAD), lambda i: (i // nb, 0, i % nb, 0))
    mspec = pl.BlockSpec((1, MEM_LEN, X_WIDTH), lambda i: (i // nb, 0, 0))
    consts = [w["gmix"], w["win"], w["gqa"], w["wq"], w["gkva"], w["wk"], w["wvt"],
              w["gq"], w["gqs"], w["gk"], w["gks"], w["gxq"]]
    hshape = jax.ShapeDtypeStruct((bsz, MLA_HEADS, seq, HEAD_PAD), BF16)
    tshape = jax.ShapeDtypeStruct((t, SSM_WIDTH), BF16)
    vtshape = jax.ShapeDtypeStruct((bsz, MLA_HEADS, seq // TK, V_ROWS, TK), BF16)
    vtspec = pl.BlockSpec((1, MLA_HEADS, tm // TK, V_ROWS, TK), lambda i: (i // nb, 0, i % nb, 0, 0))
    return pl.pallas_call(
        _front_kernel,
        grid=(t // tm,),
        in_specs=[row(D_MODEL), row(HEAD_PAD), row(HEAD_PAD), mspec, mspec]
                 + [_const_spec(c.shape) for c in consts],
        out_specs=[hspec, hspec, vtspec, row(SSM_WIDTH), row(X_WIDTH)],
        out_shape=[hshape, hshape, vtshape, tshape, tshape],
        compiler_params=_params(("arbitrary",)),
        name="front",
    )(x2, *tables, kx, vx, *consts)


def _attn_kernel(q_ref, k_ref, vt_ref, o_ref, st_scr, m_scr, acc_scr):
    qi = pl.program_id(2)
    m_scr[...] = jnp.full(m_scr.shape, -jnp.inf, F32)
    acc_scr[...] = jnp.zeros(acc_scr.shape, F32)

    def scores(kb, slot):
        r = pl.multiple_of(kb * TK, TK)
        for hh in range(ATTN_HEADS):
            st_scr[slot, hh] = _dot_nt(k_ref[0, hh, pl.ds(r, TK), :], q_ref[0, hh])

    def consume(kb, slot, masked):
        for hh in range(ATTN_HEADS):
            st = st_scr[slot, hh]
            if masked:
                key = kb * TK + lax.broadcasted_iota(jnp.int32, st.shape, 0)
                qry = qi * TQ + lax.broadcasted_iota(jnp.int32, st.shape, 1)
                st = jnp.where(key <= qry, st, -jnp.inf)
            m_old = m_scr[hh]
            m_new = jnp.maximum(m_old, jnp.max(st, axis=0, keepdims=True))
            pm = jnp.exp2(st - m_new).astype(BF16)
            alpha = jnp.exp2(m_old - m_new)
            acc_scr[hh] = alpha * acc_scr[hh] + _dot(vt_ref[0, hh, kb], pm)
            m_scr[hh] = m_new

    steps = TQ // TK
    scores(0, 0)

    def body(j, c):
        for s in range(steps):
            kb = j * steps + s
            scores(kb + 1, (s + 1) % 2)
            consume(kb, s % 2, False)
        return c

    lax.fori_loop(0, qi, body, 0)
    for s in range(steps):
        kb = qi * steps + s
        if s + 1 < steps:
            scores(kb + 1, (s + 1) % 2)
        consume(kb, s % 2, True)
    outs = []
    for hh in range(ATTN_HEADS):
        a = acc_scr[hh]
        outs.append(a[:D_V] / a[D_V:D_V + 1])
    o_ref[0] = jnp.concatenate(outs, axis=0).T.astype(BF16)


def _attention(q, k, vt):
    bsz, nh, seq, _ = q.shape
    assert TQ % TK == 0 and (TQ // TK) % 2 == 0
    nh_blk = ATTN_HEADS
    return pl.pallas_call(
        _attn_kernel,
        grid=(bsz, nh // nh_blk, seq // TQ),
        in_specs=[pl.BlockSpec((1, nh_blk, TQ, HEAD_PAD), lambda b, hg, i: (b, hg, i, 0)),
                  pl.BlockSpec((1, nh_blk, seq, HEAD_PAD), lambda b, hg, i: (b, hg, 0, 0)),
                  pl.BlockSpec((1, nh_blk, seq // TK, V_ROWS, TK), lambda b, hg, i: (b, hg, 0, 0, 0))],
        out_specs=pl.BlockSpec((1, TQ, nh_blk * D_V), lambda b, hg, i: (b, i, hg)),
        out_shape=jax.ShapeDtypeStruct((bsz, seq, nh * D_V), BF16),
        scratch_shapes=[pltpu.VMEM((2, nh_blk, TK, TQ), F32), pltpu.VMEM((nh_blk, 1, TQ), F32),
                        pltpu.VMEM((nh_blk, V_ROWS, TQ), F32)],
        compiler_params=_params(("arbitrary", "arbitrary", "arbitrary")),
        name="mla_attention",
    )(q, k, vt)


def _s5_prep_kernel(lr_ref, li_ref, ldt_ref, lrc_ref, lic_ref, ldtc_ref, bre_ref, bim_ref,
                    ptab_ref, bbre_ref, bbim_ref):
    ns = SSM_NS
    dt = jnp.exp(ldt_ref[0])
    lr = lr_ref[0] * dt
    li = li_ref[0] * dt
    k = (lax.broadcasted_iota(jnp.int32, (S5_SEG, 1), 0) + 1).astype(F32)
    mag = jnp.exp(k * lr)
    ang = k * li
    p_re = mag * jnp.cos(ang)
    p_im = mag * jnp.sin(ang)
    for r in range(S5_SEG):
        rows = slice(r * SUBLANES, (r + 1) * SUBLANES)
        ptab_ref[0, rows, :ns] = jnp.broadcast_to(p_re[r:r + 1], (SUBLANES, ns))
        ptab_ref[0, rows, ns:] = jnp.broadcast_to(p_im[r:r + 1], (SUBLANES, ns))
    dtc = jnp.exp(ldtc_ref[0])
    lrc = lrc_ref[0]
    lic = lic_ref[0]
    magc = jnp.exp(lrc * dtc)
    e_re = magc * jnp.cos(lic * dtc) - 1.0
    e_im = magc * jnp.sin(lic * dtc)
    den = lrc * lrc + lic * lic
    f_re = (e_re * lrc + e_im * lic) / den
    f_im = (e_im * lrc - e_re * lic) / den
    bre = bre_ref[0]
    bim = bim_ref[0]
    bbre_ref[0] = f_re * bre - f_im * bim
    bbim_ref[0] = f_re * bim + f_im * bre


def _s5_prep(lam_re, lam_im, log_dt, b_re, b_im):
    depth = lam_re.shape[0]
    ns = SSM_NS
    ldt = jnp.broadcast_to(log_dt[:, :, None], lam_re.shape)
    rowv = lambda a: a.reshape(depth, 1, ns)
    colv = lambda a: a.reshape(depth, ns, 1)
    bcol = lambda a: a.reshape(depth, ns, SSM_GROUP_CH)
    spec = lambda s: pl.BlockSpec((1,) + s, lambda i: (i, 0, 0))
    return pl.pallas_call(
        _s5_prep_kernel,
        grid=(depth,),
        in_specs=[spec((1, ns))] * 3 + [spec((ns, 1))] * 3 + [spec((ns, SSM_GROUP_CH))] * 2,
        out_specs=[spec((S5_CHUNK, 2 * ns)),
                   spec((ns, SSM_GROUP_CH)), spec((ns, SSM_GROUP_CH))],
        out_shape=[jax.ShapeDtypeStruct((depth, S5_CHUNK, 2 * ns), F32),
                   jax.ShapeDtypeStruct((depth, ns, SSM_GROUP_CH), F32),
                   jax.ShapeDtypeStruct((depth, ns, SSM_GROUP_CH), F32)],
        compiler_params=_params(("arbitrary",)),
        name="s5_prep",
    )(rowv(lam_re), rowv(lam_im), rowv(ldt), colv(lam_re), colv(lam_im), colv(ldt),
      bcol(b_re), bcol(b_im))


def _s5_kernel(u_ref, bmat_ref, cre_ref, cim_ref, ptab_ref, d_ref, wglu_ref, bglu_ref,
               o_ref, s_scr, carry_scr, c_scr):
    ns = SSM_NS
    cw = S5_COLS
    nb = u_ref.shape[0]
    hw = SSM_WIDTH // S5_HALVES
    hs = ns // S5_HALVES

    @pl.when(pl.program_id(0) == 0)
    def _():
        carry_scr[...] = jnp.zeros(carry_scr.shape, F32)

    for b in range(nb):
        for hf in range(S5_HALVES):
            bu = _dot(u_ref[b, :, hf * hw:(hf + 1) * hw], bmat_ref[hf])
            s_scr[b, :, hf * hs:(hf + 1) * hs] = bu[:, :hs]
            s_scr[b, :, ns + hf * hs:ns + (hf + 1) * hs] = bu[:, hs:]

    cols = lambda cg: (slice(cg * cw, (cg + 1) * cw), slice(ns + cg * cw, ns + (cg + 1) * cw))
    ncg = ns // cw

    for b in range(nb):
        z = {}
        for i in range(S5_SEG):
            rows = slice(i * SUBLANES, (i + 1) * SUBLANES)
            for cg in range(ncg):
                re, im = cols(cg)
                if i == 0:
                    z[cg] = (s_scr[b, rows, re], s_scr[b, rows, im])
                    continue
                a_re = ptab_ref[:SUBLANES, re]
                a_im = ptab_ref[:SUBLANES, im]
                z_re, z_im = z[cg]
                n_re = a_re * z_re - a_im * z_im + s_scr[b, rows, re]
                n_im = a_re * z_im + a_im * z_re + s_scr[b, rows, im]
                s_scr[b, rows, re] = n_re
                s_scr[b, rows, im] = n_im
                z[cg] = (n_re, n_im)

        c = {}
        for cg in range(ncg):
            re, im = cols(cg)
            e_re, e_im = z[cg]
            p_re = ptab_ref[S5_CHUNK - 1:S5_CHUNK, re]
            p_im = ptab_ref[S5_CHUNK - 1:S5_CHUNK, im]
            cur_re = carry_scr[b, :, re]
            cur_im = carry_scr[b, :, im]
            for j in range(SUBLANES):
                c_scr[b, j:j + 1, re] = cur_re
                c_scr[b, j:j + 1, im] = cur_im
                n_re = e_re[j:j + 1, :] + p_re * cur_re - p_im * cur_im
                n_im = e_im[j:j + 1, :] + p_re * cur_im + p_im * cur_re
                cur_re, cur_im = n_re, n_im
            carry_scr[b, :, re] = cur_re
            carry_scr[b, :, im] = cur_im
            c[cg] = (c_scr[b, :, re], c_scr[b, :, im])

        for i in range(S5_SEG):
            rows = slice(i * SUBLANES, (i + 1) * SUBLANES)
            for cg in range(ncg):
                re, im = cols(cg)
                c_re, c_im = c[cg]
                q_re = ptab_ref[rows, re]
                q_im = ptab_ref[rows, im]
                s_scr[b, rows, re] = s_scr[b, rows, re] + q_re * c_re - q_im * c_im
                s_scr[b, rows, im] = s_scr[b, rows, im] + q_re * c_im + q_im * c_re

    for b in range(nb):
        ys = []
        for hf in range(S5_HALVES):
            s_re = s_scr[b, :, hf * hs:(hf + 1) * hs].astype(BF16)
            s_im = s_scr[b, :, ns + hf * hs:ns + (hf + 1) * hs].astype(BF16)
            ys.append(_dot(s_re, cre_ref[hf]) - _dot(s_im, cim_ref[hf]))
        y = jnp.concatenate(ys, axis=1) + d_ref[...] * u_ref[b].astype(F32)
        y = jax.nn.gelu(y)
        zz = _dot(y.astype(BF16), wglu_ref[...]) + bglu_ref[...]
        o_ref[b] = (y * jax.nn.sigmoid(zz)).astype(BF16)


def _s5(u_perm, bmat, cre, cim, ptab, d, wglu, bglu):
    bsz, seq, _ = u_perm.shape
    ns = SSM_NS
    blk = pl.BlockSpec((bsz, S5_CHUNK, SSM_WIDTH), lambda j: (0, j, 0))
    consts = [bmat, cre, cim, ptab, d, wglu, bglu]
    return pl.pallas_call(
        _s5_kernel,
        grid=(seq // S5_CHUNK,),
        in_specs=[blk] + [_const_spec(c.shape) for c in consts],
        out_specs=blk,
        out_shape=jax.ShapeDtypeStruct((bsz, seq, SSM_WIDTH), BF16),
        scratch_shapes=[pltpu.VMEM((bsz, S5_CHUNK, 2 * ns), F32), pltpu.VMEM((bsz, 1, 2 * ns), F32),
                        pltpu.VMEM((bsz, SUBLANES, 2 * ns), F32)],
        compiler_params=_params(("arbitrary",)),
        name="s5_scan",
    )(u_perm, *consts)


def _merge_kernel(x_ref, ya_ref, yb_ref, yc_ref, gmix_ref, wg_ref, bg_ref, woa_ref, wob_ref,
                  woc_ref, wout_ref, o_ref):
    x = x_ref[...]
    h = _rms(x, gmix_ref[...]).astype(BF16)
    merged = None
    for br, (y_ref, wo_ref) in enumerate(((ya_ref, woa_ref), (yb_ref, wob_ref), (yc_ref, woc_ref))):
        sl = slice(br * D_MODEL, (br + 1) * D_MODEL)
        gate = jax.nn.sigmoid(_dot(h, wg_ref[:, sl]) + bg_ref[:, sl])
        term = gate * _dot(y_ref[...], wo_ref[...])
        merged = term if merged is None else merged + term
    o_ref[...] = x + _dot(merged.astype(BF16), wout_ref[...])


def _merge(x2, ya, yb, yc, w):
    t = x2.shape[0]
    tm = TM_MERGE
    row = lambda c: pl.BlockSpec((tm, c), lambda i: (i, 0))
    consts = [w["gmix"], w["wg"], w["bg"], w["woa"], w["wob"], w["woc"], w["wout"]]
    return pl.pallas_call(
        _merge_kernel,
        grid=(t // tm,),
        in_specs=[row(D_MODEL), row(MLA_WIDTH), row(SSM_WIDTH), row(X_WIDTH)]
                 + [_const_spec(c.shape) for c in consts],
        out_specs=row(D_MODEL),
        out_shape=jax.ShapeDtypeStruct((t, D_MODEL), F32),
        compiler_params=_params(("arbitrary",)),
        name="merge",
    )(x2, ya, yb, yc, *consts)


def _ffn_kernel(x_ref, g_ref, wug_ref, wuv_ref, cwg_ref, cwv_ref, cbg_ref, cbv_ref, wd_ref,
                o_ref, carry_g, carry_v, *, blocks_per_seq):
    tm = x_ref.shape[0]

    @pl.when(pl.program_id(0) % blocks_per_seq == 0)
    def _():
        carry_g[...] = jnp.zeros(carry_g.shape, F32)
        carry_v[...] = jnp.zeros(carry_v.shape, F32)

    x = x_ref[...]
    h2 = _rms(x, g_ref[...]).astype(BF16)
    row = lax.broadcasted_iota(jnp.int32, (tm, 1), 0)

    def up_proj(c):
        cs = slice(c * FFN_CHUNK, (c + 1) * FFN_CHUNK)
        return _dot(h2, wug_ref[:, cs]), _dot(h2, wuv_ref[:, cs])

    def conv(cs, up, cw_ref, cb_ref, carry):
        prev = carry[:, cs]
        carry[:, cs] = up[tm - SUBLANES:, :]
        p1 = prev[SUBLANES - 1:SUBLANES, :]
        p2 = prev[SUBLANES - 2:SUBLANES - 1, :]
        m1 = jnp.where(row == 0, p1, pltpu.roll(up, 1, 0))
        m2 = jnp.where(row == 0, p2, jnp.where(row == 1, p1, pltpu.roll(up, 2, 0)))
        return cw_ref[0:1, cs] * m2 + cw_ref[1:2, cs] * m1 + cw_ref[2:3, cs] * up + cb_ref[:, cs]

    nchunk = D_FF // FFN_CHUNK
    acc = jnp.zeros((tm, D_MODEL), F32)
    ups = up_proj(0)
    for c in range(nchunk):
        nxt = up_proj(c + 1) if c + 1 < nchunk else None
        cs = slice(c * FFN_CHUNK, (c + 1) * FFN_CHUNK)
        gv = conv(cs, ups[0], cwg_ref, cbg_ref, carry_g)
        vv = conv(cs, ups[1], cwv_ref, cbv_ref, carry_v)
        act = (gv * jax.nn.sigmoid(gv) * vv).astype(BF16)
        acc = acc + _dot(act, wd_ref[cs, :])
        ups = nxt
    o_ref[...] = x + acc


def _ffn(x2, w, seq):
    t = x2.shape[0]
    tm = TM_FFN
    row = pl.BlockSpec((tm, D_MODEL), lambda i: (i, 0))
    consts = [w["gffn"], w["wug"], w["wuv"], w["cwg"], w["cwv"], w["cbg"], w["cbv"], w["wd"]]
    return pl.pallas_call(
        functools.partial(_ffn_kernel, blocks_per_seq=seq // tm),
        grid=(t // tm,),
        in_specs=[row] + [_const_spec(c.shape) for c in consts],
        out_specs=row,
        out_shape=jax.ShapeDtypeStruct((t, D_MODEL), F32),
        scratch_shapes=[pltpu.VMEM((SUBLANES, D_FF), F32), pltpu.VMEM((SUBLANES, D_FF), F32)],
        compiler_params=_params(("arbitrary",)),
        name="ffn",
    )(x2, *consts)


def _block_diag(blocks):
    g, r, c = blocks.shape
    eye = jnp.eye(g, dtype=bool)
    full = jnp.where(eye[:, None, :, None], blocks[:, :, None, :], jnp.zeros((), blocks.dtype))
    return full.reshape(g * r, g * c)


def _pad_lanes(a, lo, width=HEAD_PAD):
    pad = [(0, 0)] * (a.ndim - 1) + [(lo, width - lo - a.shape[-1])]
    return jnp.pad(a, pad)


def _swap_rope(a):
    half = D_ROPE // 2
    lo = a[..., D_NOPE:D_NOPE + half]
    hi = a[..., D_NOPE + half:D_QK]
    return jnp.concatenate([jnp.zeros_like(a[..., :D_NOPE]), hi, lo,
                            jnp.zeros_like(a[..., D_QK:])], axis=-1)


def _layer_weights(i, p):
    w_in = p["w_in"][i]
    o = 0
    parts = {}
    for name, width in (("cq", Q_LORA), ("ckv", KV_LORA), ("kr", D_ROPE), ("u", SSM_WIDTH),
                        ("xq", X_WIDTH), ("g", N_BRANCH * D_MODEL)):
        parts[name] = w_in[:, o:o + width]
        o += width
    kr = _pad_lanes(parts["kr"], D_NOPE)
    win = jnp.concatenate([parts["cq"], parts["ckv"], kr, _swap_rope(kr),
                           parts["u"], parts["xq"]], axis=1).astype(BF16)
    wq = _pad_lanes(p["w_q_b"][i].reshape(Q_LORA, MLA_HEADS, D_QK), 0)
    wq = jnp.concatenate([wq.reshape(Q_LORA, -1), _swap_rope(wq).reshape(Q_LORA, -1)], axis=1)
    wkv = p["w_kv_b"][i].reshape(KV_LORA, MLA_HEADS, D_NOPE + D_V)
    wk = _pad_lanes(wkv[:, :, :D_NOPE], 0).reshape(KV_LORA, -1)
    wvt = wkv[:, :, D_NOPE:].reshape(KV_LORA, MLA_WIDTH).T
    row = lambda a: a.reshape(1, -1)
    gq = _pad_lanes(row(p["q_norm_g"][i]), 0)
    gk = _pad_lanes(row(p["k_norm_g"][i]), 0)
    w_up = p["w_up"][i]
    conv_w = p["conv_w"][i]
    conv_b = p["conv_b"][i]
    return {
        "gmix": row(p["norm_mix_g"][i]),
        "win": win,
        "gqa": row(p["q_a_norm_g"][i]),
        "wq": wq.astype(BF16),
        "gkva": row(p["kv_a_norm_g"][i]),
        "wk": wk.astype(BF16),
        "wvt": wvt.astype(BF16),
        "gq": gq,
        "gqs": _swap_rope(gq),
        "gk": gk,
        "gks": _swap_rope(gk),
        "gxq": row(p["xq_norm_g"][i]),
        "wg": parts["g"].astype(BF16),
        "bg": row(p["b_gate"][i]),
        "woa": p["w_o_mla"][i].astype(BF16),
        "wob": p["w_o_ssm"][i].astype(BF16),
        "woc": p["w_o_cross"][i].astype(BF16),
        "wout": p["w_out"][i].astype(BF16),
        "gffn": row(p["norm_ffn_g"][i]),
        "wug": w_up[:, :D_FF].astype(BF16),
        "wuv": w_up[:, D_FF:].astype(BF16),
        "cwg": conv_w[:, :D_FF],
        "cwv": conv_w[:, D_FF:],
        "cbg": row(conv_b[:D_FF]),
        "cbv": row(conv_b[D_FF:]),
        "wd": p["w_down"][i].astype(BF16),
    }


def _segment_major(a, bsz, seq):
    c = a.shape[-1]
    a = a.reshape(bsz, seq // S5_CHUNK, SUBLANES, S5_SEG, c)
    return a.transpose(0, 1, 3, 2, 4).reshape(bsz, seq, c)


def _time_major(a, bsz, seq):
    c = a.shape[-1]
    a = a.reshape(bsz, seq // S5_CHUNK, S5_SEG, SUBLANES, c)
    return a.transpose(0, 1, 3, 2, 4).reshape(bsz, seq, c)


def kernel(x, mem, positions, norm_mix_g, w_in, q_a_norm_g, w_q_b, kv_a_norm_g, w_kv_b, q_norm_g, k_norm_g, w_o_mla, ssm_lambda_re, ssm_lambda_im, ssm_log_dt, ssm_b_re, ssm_b_im, ssm_c_re, ssm_c_im, ssm_d, w_glu, b_glu, w_o_ssm, mem_norm_g, w_mem_kv, xq_norm_g, xk_norm_g, w_o_cross, b_gate, w_out, norm_ffn_g, w_up, conv_w, conv_b, w_down):
    p = dict(norm_mix_g=norm_mix_g, w_in=w_in, q_a_norm_g=q_a_norm_g, w_q_b=w_q_b,
             kv_a_norm_g=kv_a_norm_g, w_kv_b=w_kv_b, q_norm_g=q_norm_g, k_norm_g=k_norm_g,
             w_o_mla=w_o_mla, w_o_ssm=w_o_ssm, w_o_cross=w_o_cross, b_gate=b_gate, w_out=w_out,
             norm_ffn_g=norm_ffn_g, w_up=w_up, conv_w=conv_w, conv_b=conv_b, w_down=w_down,
             xq_norm_g=xq_norm_g)
    bsz, seq, _ = x.shape
    t = bsz * seq
    assert seq % max(TM_FRONT, TM_MERGE, TM_FFN, TQ, S5_CHUNK) == 0 and t % ROPE_ROWS == 0

    inv_freq = ROPE_THETA ** (-jnp.arange(0, D_ROPE, 2, dtype=F32) / D_ROPE)
    invf = jnp.concatenate([jnp.zeros((D_NOPE,), F32), inv_freq, inv_freq,
                            jnp.zeros((HEAD_PAD - D_QK,), F32)]).reshape(1, HEAD_PAD)
    tables = _rope_tables(positions.reshape(t, 1), invf)

    ptab_all, bbre_all, bbim_all = _s5_prep(ssm_lambda_re, ssm_lambda_im, ssm_log_dt,
                                                   ssm_b_re, ssm_b_im)

    x2 = x.reshape(t, D_MODEL)
    for i in range(DEPTH):
        w = _layer_weights(i, p)
        kx, vx = _memkv(mem, mem_norm_g[i].reshape(1, -1), w_mem_kv[i].astype(BF16),
                        xk_norm_g[i].reshape(1, -1))
        q, k, vt, u, yc = _front(x2, tables, kx, vx, w, bsz, seq)
        ya = _attention(q, k, vt).reshape(t, MLA_WIDTH)

        shp = (S5_HALVES, SSM_GROUPS // S5_HALVES, SSM_STATE, SSM_GROUP_CH)
        bd = jax.vmap(_block_diag)
        bmat = jnp.concatenate(
            [bd(bbre_all[i].reshape(shp).transpose(0, 1, 3, 2)),
             bd(bbim_all[i].reshape(shp).transpose(0, 1, 3, 2))], axis=2).astype(BF16)
        cshp = (S5_HALVES, SSM_GROUPS // S5_HALVES, SSM_GROUP_CH, SSM_STATE)
        cre = bd(ssm_c_re[i].reshape(cshp).transpose(0, 1, 3, 2)).astype(BF16)
        cim = bd(ssm_c_im[i].reshape(cshp).transpose(0, 1, 3, 2)).astype(BF16)
        yb = _s5(_segment_major(u.reshape(bsz, seq, SSM_WIDTH), bsz, seq), bmat, cre, cim,
                 ptab_all[i], ssm_d[i].reshape(1, SSM_WIDTH), w_glu[i].astype(BF16),
                 b_glu[i].reshape(1, SSM_WIDTH))
        yb = _time_major(yb, bsz, seq).reshape(t, SSM_WIDTH)

        x2 = _merge(x2, ya, yb, yc, w)
        x2 = _ffn(x2, w, seq)
    return x2.reshape(bsz, seq, D_MODEL)
```

```python
import functools
import math

import jax
import jax.numpy as jnp
from jax import lax
from jax.experimental import pallas as pl
from jax.experimental.pallas import tpu as pltpu

F32 = jnp.float32
BF16 = jnp.bfloat16

D_MODEL = 1024
DEPTH = 2
MEM_LEN = 256
EPS = 1e-6
MLA_HEADS = 8
Q_LORA = 384
KV_LORA = 256
D_NOPE = 64
D_ROPE = 32
D_QK = D_NOPE + D_ROPE
D_V = 64
MLA_WIDTH = MLA_HEADS * D_V
ROPE_THETA = 10000.0
SSM_GROUPS = 32
SSM_GROUP_CH = 16
SSM_WIDTH = SSM_GROUPS * SSM_GROUP_CH
SSM_STATE = 64
SSM_NS = SSM_GROUPS * SSM_STATE
X_HEADS = 4
X_HEAD_DIM = 128
X_WIDTH = X_HEADS * X_HEAD_DIM
N_BRANCH = 3
D_FF = 2816
CONV_WIDTH = 3

LANES = 128
SUBLANES = 8
HEAD_PAD = LANES

TM_FRONT = 512
TM_MIX = 512
FFN_CHUNK = 256
TQ = 512
TK = 256
ATTN_HEADS = 4
V_ROWS = D_V + 16
S5_SEG = 32
S5_CHUNK = SUBLANES * S5_SEG
S5_COLS = 512
S5_HALVES = 2
ROPE_ROWS = 1024

VMEM_LIMIT = 56 * 1024 * 1024


def _layer_spec(arr, layer):
    nd = arr.ndim - 1
    return pl.BlockSpec((None,) + arr.shape[1:], lambda *_: (layer,) + (0,) * nd,
                        pipeline_mode=pl.Buffered(1))


def _rms(x, g):
    return x * lax.rsqrt(jnp.mean(x * x, axis=-1, keepdims=True) + EPS) * g


def _dot(a, b):
    return jnp.dot(a, b, preferred_element_type=F32)


def _dot_nt(a, b):
    return lax.dot_general(a, b, (((1,), (1,)), ((), ())), preferred_element_type=F32)


def _params(sem, limit=VMEM_LIMIT):
    return pltpu.CompilerParams(dimension_semantics=sem, vmem_limit_bytes=limit)


def _rope_kernel(pos_ref, invf_ref, cc_ref, ss_ref):
    ang = pos_ref[...].astype(F32) * invf_ref[...]
    c = jnp.cos(ang)
    s = jnp.sin(ang)
    lane = lax.broadcasted_iota(jnp.int32, ang.shape, 1)
    half = D_ROPE // 2
    cc_ref[...] = jnp.where(lane < D_QK, c, 0.0)
    ss_ref[...] = jnp.where((lane >= D_NOPE) & (lane < D_NOPE + half), -s,
                            jnp.where((lane >= D_NOPE + half) & (lane < D_QK), s, 0.0))


def _rope_tables(pos_col, invf):
    t = pos_col.shape[0]
    out = jax.ShapeDtypeStruct((t, HEAD_PAD), F32)
    spec = pl.BlockSpec((ROPE_ROWS, HEAD_PAD), lambda i: (i, 0))
    return pl.pallas_call(
        _rope_kernel,
        grid=(t // ROPE_ROWS,),
        in_specs=[pl.BlockSpec((ROPE_ROWS, 1), lambda i: (i, 0)), _layer_spec(invf, 0)],
        out_specs=[spec, spec],
        out_shape=[out, out],
        compiler_params=_params(("arbitrary",)),
        name="rope_tables",
    )(pos_col, invf)


def _memkv_kernel(mem_ref, g_ref, w_ref, gk_ref, k_ref, v_ref):
    m = _rms(mem_ref[0], g_ref[...]).astype(BF16)
    kv = _dot(m, w_ref[...])
    for h in range(X_HEADS):
        sl = slice(h * X_HEAD_DIM, (h + 1) * X_HEAD_DIM)
        k_ref[0, :, sl] = _rms(kv[:, sl], gk_ref[...]).astype(BF16)
    v_ref[0] = kv[:, X_WIDTH:].astype(BF16)


def _memkv(mem, g, w, gk, layer):
    b = mem.shape[0]
    out = jax.ShapeDtypeStruct((b, MEM_LEN, X_WIDTH), BF16)
    spec = pl.BlockSpec((1, MEM_LEN, X_WIDTH), lambda i: (i, 0, 0))
    return pl.pallas_call(
        _memkv_kernel,
        grid=(b,),
        in_specs=[pl.BlockSpec((1, MEM_LEN, D_MODEL), lambda i: (i, 0, 0)),
                  _layer_spec(g, layer), _layer_spec(w, layer), _layer_spec(gk, layer)],
        out_specs=[spec, spec],
        out_shape=[out, out],
        compiler_params=_params(("arbitrary",)),
        name="mem_kv",
    )(mem, g, w, gk)


def _front_kernel(x_ref, cc_ref, ss_ref, kx_ref, vx_ref, gmix_ref, win_ref,
                  gqa_ref, wq_ref, gkva_ref, wk_ref, wvt_ref, gq_ref, gqs_ref, gk_ref, gks_ref,
                  gxq_ref, q_ref, k_ref, vt_ref, u_ref, yc_ref):
    tm = x_ref.shape[0]
    h = _rms(x_ref[...], gmix_ref[...]).astype(BF16)
    p = _dot(h, win_ref[...])
    o_ckv = Q_LORA
    o_kr = o_ckv + KV_LORA
    o_krs = o_kr + HEAD_PAD
    o_u = o_krs + HEAD_PAD
    o_xq = o_u + SSM_WIDTH
    cq = p[:, :o_ckv]
    ckv = p[:, o_ckv:o_kr]
    kr = p[:, o_kr:o_krs]
    krs = p[:, o_krs:o_u]
    u_ref[...] = p[:, o_u:o_xq].astype(BF16)
    xq = p[:, o_xq:]

    cqn = _rms(cq, gqa_ref[...]).astype(BF16)
    ckvn = _rms(ckv, gkva_ref[...]).astype(BF16)

    vt = _dot_nt(wvt_ref[...], ckvn)
    ones = jnp.ones((V_ROWS - D_V, TK), BF16)
    for hd in range(MLA_HEADS):
        for c in range(tm // TK):
            vt_ref[0, hd, c, :D_V, :] = vt[hd * D_V:(hd + 1) * D_V, c * TK:(c + 1) * TK].astype(BF16)
            vt_ref[0, hd, c, D_V:, :] = ones

    cc = cc_ref[...]
    ss = ss_ref[...]
    nq = MLA_HEADS * HEAD_PAD

    def inv_rms(a):
        return lax.rsqrt(jnp.sum(a * a, axis=-1, keepdims=True) * (1.0 / D_QK) + EPS)

    qa = _dot(cqn, wq_ref[...])
    gc = gq_ref[...] * cc
    gs = gqs_ref[...] * ss
    scale = D_QK ** -0.5 * math.log2(math.e)
    for hd in range(MLA_HEADS):
        a = qa[:, hd * HEAD_PAD:(hd + 1) * HEAD_PAD]
        a_s = qa[:, nq + hd * HEAD_PAD:nq + (hd + 1) * HEAD_PAD]
        q_ref[0, hd] = ((inv_rms(a) * scale) * (a * gc + a_s * gs)).astype(BF16)

    ka = _dot(ckvn, wk_ref[...])
    gc = gk_ref[...] * cc
    rot = krs * (gks_ref[...] * ss)
    for hd in range(MLA_HEADS):
        a = ka[:, hd * HEAD_PAD:(hd + 1) * HEAD_PAD] + kr
        k_ref[0, hd] = (inv_rms(a) * (a * gc + rot)).astype(BF16)

    xscale = X_HEAD_DIM ** -0.5
    for hd in range(X_HEADS):
        sl = slice(hd * X_HEAD_DIM, (hd + 1) * X_HEAD_DIM)
        qx = (_rms(xq[:, sl], gxq_ref[...]) * xscale).astype(BF16)
        s = _dot_nt(qx, kx_ref[0, :, sl])
        pm = jnp.exp(s - jnp.max(s, axis=-1, keepdims=True))
        l = jnp.sum(pm, axis=-1, keepdims=True)
        yc_ref[:, sl] = (_dot(pm.astype(BF16), vx_ref[0, :, sl]) / l).astype(BF16)


def _front(x2, tables, kx, vx, w, layer, bsz, seq):
    t = bsz * seq
    tm = TM_FRONT
    nb = seq // tm
    row = lambda c: pl.BlockSpec((tm, c), lambda i: (i, 0))
    hspec = pl.BlockSpec((1, MLA_HEADS, tm, HEAD_PAD), lambda i: (i // nb, 0, i % nb, 0))
    mspec = pl.BlockSpec((1, MEM_LEN, X_WIDTH), lambda i: (i // nb, 0, 0))
    consts = [w["gmix"], w["win"], w["gqa"], w["wq"], w["gkva"], w["wk"], w["wvt"],
              w["gq"], w["gqs"], w["gk"], w["gks"], w["gxq"]]
    hshape = jax.ShapeDtypeStruct((bsz, MLA_HEADS, seq, HEAD_PAD), BF16)
    tshape = jax.ShapeDtypeStruct((t, SSM_WIDTH), BF16)
    vtshape = jax.ShapeDtypeStruct((bsz, MLA_HEADS, seq // TK, V_ROWS, TK), BF16)
    vtspec = pl.BlockSpec((1, MLA_HEADS, tm // TK, V_ROWS, TK), lambda i: (i // nb, 0, i % nb, 0, 0))
    return pl.pallas_call(
        _front_kernel,
        grid=(t // tm,),
        in_specs=[row(D_MODEL), row(HEAD_PAD), row(HEAD_PAD), mspec, mspec]
                 + [_layer_spec(c, layer) for c in consts],
        out_specs=[hspec, hspec, vtspec, row(SSM_WIDTH), row(X_WIDTH)],
        out_shape=[hshape, hshape, vtshape, tshape, tshape],
        compiler_params=_params(("arbitrary",)),
        name="front",
    )(x2, *tables, kx, vx, *consts)


def _attn_kernel(q_ref, k_ref, vt_ref, o_ref, st_scr, m_scr, acc_scr):
    qi = pl.program_id(2)
    m_scr[...] = jnp.full(m_scr.shape, -jnp.inf, F32)
    acc_scr[...] = jnp.zeros(acc_scr.shape, F32)

    def scores(kb, slot):
        r = pl.multiple_of(kb * TK, TK)
        for hh in range(ATTN_HEADS):
            st_scr[slot, hh] = _dot_nt(k_ref[0, hh, pl.ds(r, TK), :], q_ref[0, hh])

    def consume(kb, slot, masked):
        for hh in range(ATTN_HEADS):
            st = st_scr[slot, hh]
            if masked:
                key = kb * TK + lax.broadcasted_iota(jnp.int32, st.shape, 0)
                qry = qi * TQ + lax.broadcasted_iota(jnp.int32, st.shape, 1)
                st = jnp.where(key <= qry, st, -jnp.inf)
            m_old = m_scr[hh]
            m_new = jnp.maximum(m_old, jnp.max(st, axis=0, keepdims=True))
            pm = jnp.exp2(st - m_new).astype(BF16)
            alpha = jnp.exp2(m_old - m_new)
            acc_scr[hh] = alpha * acc_scr[hh] + _dot(vt_ref[0, hh, kb], pm)
            m_scr[hh] = m_new

    steps = TQ // TK
    scores(0, 0)

    def body(j, c):
        for s in range(steps):
            kb = j * steps + s
            scores(kb + 1, (s + 1) % 2)
            consume(kb, s % 2, False)
        return c

    lax.fori_loop(0, qi, body, 0)
    for s in range(steps):
        kb = qi * steps + s
        if s + 1 < steps:
            scores(kb + 1, (s + 1) % 2)
        consume(kb, s % 2, True)
    outs = []
    for hh in range(ATTN_HEADS):
        a = acc_scr[hh]
        outs.append(a[:D_V] / a[D_V:D_V + 1])
    o_ref[0] = jnp.concatenate(outs, axis=0).T.astype(BF16)


def _attention(q, k, vt):
    bsz, nh, seq, _ = q.shape
    assert TQ % TK == 0 and (TQ // TK) % 2 == 0
    nh_blk = ATTN_HEADS
    return pl.pallas_call(
        _attn_kernel,
        grid=(bsz, nh // nh_blk, seq // TQ),
        in_specs=[pl.BlockSpec((1, nh_blk, TQ, HEAD_PAD), lambda b, hg, i: (b, hg, i, 0)),
                  pl.BlockSpec((1, nh_blk, seq, HEAD_PAD), lambda b, hg, i: (b, hg, 0, 0)),
                  pl.BlockSpec((1, nh_blk, seq // TK, V_ROWS, TK), lambda b, hg, i: (b, hg, 0, 0, 0))],
        out_specs=pl.BlockSpec((1, TQ, nh_blk * D_V), lambda b, hg, i: (b, i, hg)),
        out_shape=jax.ShapeDtypeStruct((bsz, seq, nh * D_V), BF16),
        scratch_shapes=[pltpu.VMEM((2, nh_blk, TK, TQ), F32), pltpu.VMEM((nh_blk, 1, TQ), F32),
                        pltpu.VMEM((nh_blk, V_ROWS, TQ), F32)],
        compiler_params=_params(("arbitrary", "arbitrary", "arbitrary")),
        name="mla_attention",
    )(q, k, vt)


def _s5_prep_kernel(lr_ref, li_ref, ldt_ref, lrc_ref, lic_ref, ldtc_ref, bre_ref, bim_ref,
                    ptab_ref, bbre_ref, bbim_ref):
    ns = SSM_NS
    dt = jnp.exp(ldt_ref[0])
    lr = lr_ref[0] * dt
    li = li_ref[0] * dt
    k = (lax.broadcasted_iota(jnp.int32, (S5_SEG, 1), 0) + 1).astype(F32)
    mag = jnp.exp(k * lr)
    ang = k * li
    p_re = mag * jnp.cos(ang)
    p_im = mag * jnp.sin(ang)
    for r in range(S5_SEG):
        rows = slice(r * SUBLANES, (r + 1) * SUBLANES)
        ptab_ref[0, rows, :ns] = jnp.broadcast_to(p_re[r:r + 1], (SUBLANES, ns))
        ptab_ref[0, rows, ns:] = jnp.broadcast_to(p_im[r:r + 1], (SUBLANES, ns))
    dtc = jnp.exp(ldtc_ref[0])
    lrc = lrc_ref[0]
    lic = lic_ref[0]
    magc = jnp.exp(lrc * dtc)
    e_re = magc * jnp.cos(lic * dtc) - 1.0
    e_im = magc * jnp.sin(lic * dtc)
    den = lrc * lrc + lic * lic
    f_re = (e_re * lrc + e_im * lic) / den
    f_im = (e_im * lrc - e_re * lic) / den
    bre = bre_ref[0]
    bim = bim_ref[0]
    bbre_ref[0] = f_re * bre - f_im * bim
    bbim_ref[0] = f_re * bim + f_im * bre


def _s5_prep(lam_re, lam_im, log_dt, b_re, b_im):
    depth = lam_re.shape[0]
    ns = SSM_NS
    ldt = jnp.broadcast_to(log_dt[:, :, None], lam_re.shape)
    rowv = lambda a: a.reshape(depth, 1, ns)
    colv = lambda a: a.reshape(depth, ns, 1)
    bcol = lambda a: a.reshape(depth, ns, SSM_GROUP_CH)
    spec = lambda s: pl.BlockSpec((1,) + s, lambda i: (i, 0, 0))
    return pl.pallas_call(
        _s5_prep_kernel,
        grid=(depth,),
        in_specs=[spec((1, ns))] * 3 + [spec((ns, 1))] * 3 + [spec((ns, SSM_GROUP_CH))] * 2,
        out_specs=[spec((S5_CHUNK, 2 * ns)),
                   spec((ns, SSM_GROUP_CH)), spec((ns, SSM_GROUP_CH))],
        out_shape=[jax.ShapeDtypeStruct((depth, S5_CHUNK, 2 * ns), F32),
                   jax.ShapeDtypeStruct((depth, ns, SSM_GROUP_CH), F32),
                   jax.ShapeDtypeStruct((depth, ns, SSM_GROUP_CH), F32)],
        compiler_params=_params(("arbitrary",)),
        name="s5_prep",
    )(rowv(lam_re), rowv(lam_im), rowv(ldt), colv(lam_re), colv(lam_im), colv(ldt),
      bcol(b_re), bcol(b_im))


def _s5_kernel(u_ref, bmat_ref, cre_ref, cim_ref, ptab_ref, d_ref, wglu_ref, bglu_ref,
               o_ref, s_scr, carry_scr, c_scr):
    ns = SSM_NS
    cw = S5_COLS
    nb = u_ref.shape[0]
    hw = SSM_WIDTH // S5_HALVES
    hs = ns // S5_HALVES

    @pl.when(pl.program_id(0) == 0)
    def _():
        carry_scr[...] = jnp.zeros(carry_scr.shape, F32)

    for b in range(nb):
        for hf in range(S5_HALVES):
            bu = _dot(u_ref[b, :, hf * hw:(hf + 1) * hw], bmat_ref[hf])
            s_scr[b, :, hf * hs:(hf + 1) * hs] = bu[:, :hs]
            s_scr[b, :, ns + hf * hs:ns + (hf + 1) * hs] = bu[:, hs:]

    cols = lambda cg: (slice(cg * cw, (cg + 1) * cw), slice(ns + cg * cw, ns + (cg + 1) * cw))
    ncg = ns // cw

    for b in range(nb):
        z = {}
        for i in range(S5_SEG):
            rows = slice(i * SUBLANES, (i + 1) * SUBLANES)
            for cg in range(ncg):
                re, im = cols(cg)
                if i == 0:
                    z[cg] = (s_scr[b, rows, re], s_scr[b, rows, im])
                    continue
                a_re = ptab_ref[:SUBLANES, re]
                a_im = ptab_ref[:SUBLANES, im]
                z_re, z_im = z[cg]
                n_re = a_re * z_re - a_im * z_im + s_scr[b, rows, re]
                n_im = a_re * z_im + a_im * z_re + s_scr[b, rows, im]
                s_scr[b, rows, re] = n_re
                s_scr[b, rows, im] = n_im
                z[cg] = (n_re, n_im)

        c = {}
        for cg in range(ncg):
            re, im = cols(cg)
            e_re, e_im = z[cg]
            p_re = ptab_ref[S5_CHUNK - 1:S5_CHUNK, re]
            p_im = ptab_ref[S5_CHUNK - 1:S5_CHUNK, im]
            cur_re = carry_scr[b, :, re]
            cur_im = carry_scr[b, :, im]
            for j in range(SUBLANES):
                c_scr[b, j:j + 1, re] = cur_re
                c_scr[b, j:j + 1, im] = cur_im
                n_re = e_re[j:j + 1, :] + p_re * cur_re - p_im * cur_im
                n_im = e_im[j:j + 1, :] + p_re * cur_im + p_im * cur_re
                cur_re, cur_im = n_re, n_im
            carry_scr[b, :, re] = cur_re
            carry_scr[b, :, im] = cur_im
            c[cg] = (c_scr[b, :, re], c_scr[b, :, im])

        for i in range(S5_SEG):
            rows = slice(i * SUBLANES, (i + 1) * SUBLANES)
            for cg in range(ncg):
                re, im = cols(cg)
                c_re, c_im = c[cg]
                q_re = ptab_ref[rows, re]
                q_im = ptab_ref[rows, im]
                s_scr[b, rows, re] = s_scr[b, rows, re] + q_re * c_re - q_im * c_im
                s_scr[b, rows, im] = s_scr[b, rows, im] + q_re * c_im + q_im * c_re

    for b in range(nb):
        ys = []
        for hf in range(S5_HALVES):
            s_re = s_scr[b, :, hf * hs:(hf + 1) * hs].astype(BF16)
            s_im = s_scr[b, :, ns + hf * hs:ns + (hf + 1) * hs].astype(BF16)
            ys.append(_dot(s_re, cre_ref[hf]) - _dot(s_im, cim_ref[hf]))
        y = jnp.concatenate(ys, axis=1) + d_ref[...] * u_ref[b].astype(F32)
        y = jax.nn.gelu(y)
        zz = _dot(y.astype(BF16), wglu_ref[...]) + bglu_ref[...]
        o_ref[b] = (y * jax.nn.sigmoid(zz)).astype(BF16)


def _s5(u_perm, bmat, cre, cim, ptab, d, wglu, bglu, layer):
    bsz, seq, _ = u_perm.shape
    ns = SSM_NS
    blk = pl.BlockSpec((bsz, S5_CHUNK, SSM_WIDTH), lambda j: (0, j, 0))
    consts = [bmat, cre, cim, ptab, d, wglu, bglu]
    return pl.pallas_call(
        _s5_kernel,
        grid=(seq // S5_CHUNK,),
        in_specs=[blk] + [_layer_spec(c, layer) for c in consts],
        out_specs=blk,
        out_shape=jax.ShapeDtypeStruct((bsz, seq, SSM_WIDTH), BF16),
        scratch_shapes=[pltpu.VMEM((bsz, S5_CHUNK, 2 * ns), F32), pltpu.VMEM((bsz, 1, 2 * ns), F32),
                        pltpu.VMEM((bsz, SUBLANES, 2 * ns), F32)],
        compiler_params=_params(("arbitrary",)),
        name="s5_scan",
    )(u_perm, *consts)


def _mix_kernel(x_ref, ya_ref, yb_ref, yc_ref, gmix_ref, wg_ref, bg_ref, woa_ref, wob_ref,
                woc_ref, wout_ref, g_ref, wug_ref, wuv_ref, cwg_ref, cwv_ref, cbg_ref, cbv_ref,
                wd_ref, o_ref, carry_g, carry_v, *, blocks_per_seq):
    tm = x_ref.shape[0]

    @pl.when(pl.program_id(0) % blocks_per_seq == 0)
    def _():
        carry_g[...] = jnp.zeros(carry_g.shape, F32)
        carry_v[...] = jnp.zeros(carry_v.shape, F32)

    x = x_ref[...]
    h = _rms(x, gmix_ref[...]).astype(BF16)
    merged = None
    for br, (y_ref, wo_ref) in enumerate(((ya_ref, woa_ref), (yb_ref, wob_ref), (yc_ref, woc_ref))):
        sl = slice(br * D_MODEL, (br + 1) * D_MODEL)
        gate = jax.nn.sigmoid(_dot(h, wg_ref[:, sl]) + bg_ref[:, sl])
        term = gate * _dot(y_ref[...], wo_ref[...])
        merged = term if merged is None else merged + term
    x = x + _dot(merged.astype(BF16), wout_ref[...])

    h2 = _rms(x, g_ref[...]).astype(BF16)
    row = lax.broadcasted_iota(jnp.int32, (tm, 1), 0)

    def up_proj(c):
        cs = slice(c * FFN_CHUNK, (c + 1) * FFN_CHUNK)
        return _dot(h2, wug_ref[:, cs]), _dot(h2, wuv_ref[:, cs])

    def conv(cs, up, cw_ref, cb_ref, carry):
        prev = carry[:, cs]
        carry[:, cs] = up[tm - SUBLANES:, :]
        p1 = prev[SUBLANES - 1:SUBLANES, :]
        p2 = prev[SUBLANES - 2:SUBLANES - 1, :]
        m1 = jnp.where(row == 0, p1, pltpu.roll(up, 1, 0))
        m2 = jnp.where(row == 0, p2, jnp.where(row == 1, p1, pltpu.roll(up, 2, 0)))
        return cw_ref[0:1, cs] * m2 + cw_ref[1:2, cs] * m1 + cw_ref[2:3, cs] * up + cb_ref[:, cs]

    nchunk = D_FF // FFN_CHUNK
    acc = jnp.zeros((tm, D_MODEL), F32)
    ups = up_proj(0)
    for c in range(nchunk):
        nxt = up_proj(c + 1) if c + 1 < nchunk else None
        cs = slice(c * FFN_CHUNK, (c + 1) * FFN_CHUNK)
        gv = conv(cs, ups[0], cwg_ref, cbg_ref, carry_g)
        vv = conv(cs, ups[1], cwv_ref, cbv_ref, carry_v)
        act = (gv * jax.nn.sigmoid(gv) * vv).astype(BF16)
        acc = acc + _dot(act, wd_ref[cs, :])
        ups = nxt
    o_ref[...] = x + acc


def _mix(x2, ya, yb, yc, w, layer, seq):
    t = x2.shape[0]
    tm = TM_MIX
    row = lambda c: pl.BlockSpec((tm, c), lambda i: (i, 0))
    consts = [w["gmix"], w["wg"], w["bg"], w["woa"], w["wob"], w["woc"], w["wout"],
              w["gffn"], w["wug"], w["wuv"], w["cwg"], w["cwv"], w["cbg"], w["cbv"], w["wd"]]
    return pl.pallas_call(
        functools.partial(_mix_kernel, blocks_per_seq=seq // tm),
        grid=(t // tm,),
        in_specs=[row(D_MODEL), row(MLA_WIDTH), row(SSM_WIDTH), row(X_WIDTH)]
                 + [_layer_spec(c, layer) for c in consts],
        out_specs=row(D_MODEL),
        out_shape=jax.ShapeDtypeStruct((t, D_MODEL), F32),
        scratch_shapes=[pltpu.VMEM((SUBLANES, D_FF), F32), pltpu.VMEM((SUBLANES, D_FF), F32)],
        compiler_params=_params(("arbitrary",)),
        name="mix",
    )(x2, ya, yb, yc, *consts)


def _block_diag(blocks):
    g, r, c = blocks.shape
    eye = jnp.eye(g, dtype=bool)
    full = jnp.where(eye[:, None, :, None], blocks[:, :, None, :], jnp.zeros((), blocks.dtype))
    return full.reshape(g * r, g * c)


def _pad_lanes(a, lo, width=HEAD_PAD):
    pad = [(0, 0)] * (a.ndim - 1) + [(lo, width - lo - a.shape[-1])]
    return jnp.pad(a, pad)


def _swap_rope(a):
    half = D_ROPE // 2
    lo = a[..., D_NOPE:D_NOPE + half]
    hi = a[..., D_NOPE + half:D_QK]
    return jnp.concatenate([jnp.zeros_like(a[..., :D_NOPE]), hi, lo,
                            jnp.zeros_like(a[..., D_QK:])], axis=-1)


def _stacked_weights(p):
    depth = p["w_in"].shape[0]
    w_in = p["w_in"]
    o = 0
    parts = {}
    for name, width in (("cq", Q_LORA), ("ckv", KV_LORA), ("kr", D_ROPE), ("u", SSM_WIDTH),
                        ("xq", X_WIDTH), ("g", N_BRANCH * D_MODEL)):
        parts[name] = w_in[:, :, o:o + width]
        o += width
    kr = _pad_lanes(parts["kr"], D_NOPE)
    win = jnp.concatenate([parts["cq"], parts["ckv"], kr, _swap_rope(kr),
                           parts["u"], parts["xq"]], axis=2).astype(BF16)
    wq = _pad_lanes(p["w_q_b"].reshape(depth, Q_LORA, MLA_HEADS, D_QK), 0)
    wq = jnp.concatenate([wq.reshape(depth, Q_LORA, -1),
                          _swap_rope(wq).reshape(depth, Q_LORA, -1)], axis=2)
    wkv = p["w_kv_b"].reshape(depth, KV_LORA, MLA_HEADS, D_NOPE + D_V)
    wk = _pad_lanes(wkv[..., :D_NOPE], 0).reshape(depth, KV_LORA, -1)
    wvt = wkv[..., D_NOPE:].reshape(depth, KV_LORA, MLA_WIDTH).transpose(0, 2, 1)
    row = lambda a: a.reshape(depth, 1, -1)
    gq = _pad_lanes(row(p["q_norm_g"]), 0)
    gk = _pad_lanes(row(p["k_norm_g"]), 0)
    w_up = p["w_up"]
    conv_w = p["conv_w"]
    conv_b = p["conv_b"]
    shp = (depth, S5_HALVES, SSM_GROUPS // S5_HALVES, SSM_STATE, SSM_GROUP_CH)
    cshp = (depth, S5_HALVES, SSM_GROUPS // S5_HALVES, SSM_GROUP_CH, SSM_STATE)
    bd = jax.vmap(jax.vmap(_block_diag))
    return {
        "gmix": row(p["norm_mix_g"]),
        "win": win,
        "gqa": row(p["q_a_norm_g"]),
        "wq": wq.astype(BF16),
        "gkva": row(p["kv_a_norm_g"]),
        "wk": wk.astype(BF16),
        "wvt": wvt.astype(BF16),
        "gq": gq,
        "gqs": _swap_rope(gq),
        "gk": gk,
        "gks": _swap_rope(gk),
        "gxq": row(p["xq_norm_g"]),
        "wg": parts["g"].astype(BF16),
        "bg": row(p["b_gate"]),
        "woa": p["w_o_mla"].astype(BF16),
        "wob": p["w_o_ssm"].astype(BF16),
        "woc": p["w_o_cross"].astype(BF16),
        "wout": p["w_out"].astype(BF16),
        "gffn": row(p["norm_ffn_g"]),
        "wug": w_up[:, :, :D_FF].astype(BF16),
        "wuv": w_up[:, :, D_FF:].astype(BF16),
        "cwg": conv_w[:, :, :D_FF],
        "cwv": conv_w[:, :, D_FF:],
        "cbg": row(conv_b[:, :D_FF]),
        "cbv": row(conv_b[:, D_FF:]),
        "wd": p["w_down"].astype(BF16),
        "gmem": row(p["mem_norm_g"]),
        "wmem": p["w_mem_kv"].astype(BF16),
        "gxk": row(p["xk_norm_g"]),
        "bmat": jnp.concatenate(
            [bd(p["bb_re"].reshape(shp).transpose(0, 1, 2, 4, 3)),
             bd(p["bb_im"].reshape(shp).transpose(0, 1, 2, 4, 3))], axis=3).astype(BF16),
        "cre": bd(p["ssm_c_re"].reshape(cshp).transpose(0, 1, 2, 4, 3)).astype(BF16),
        "cim": bd(p["ssm_c_im"].reshape(cshp).transpose(0, 1, 2, 4, 3)).astype(BF16),
        "ssm_d": row(p["ssm_d"]),
        "wglu": p["w_glu"].astype(BF16),
        "bglu": row(p["b_glu"]),
    }


def _segment_major(a, bsz, seq):
    c = a.shape[-1]
    a = a.reshape(bsz, seq // S5_CHUNK, SUBLANES, S5_SEG, c)
    return a.transpose(0, 1, 3, 2, 4).reshape(bsz, seq, c)


def _time_major(a, bsz, seq):
    c = a.shape[-1]
    a = a.reshape(bsz, seq // S5_CHUNK, S5_SEG, SUBLANES, c)
    return a.transpose(0, 1, 3, 2, 4).reshape(bsz, seq, c)


def kernel(x, mem, positions, norm_mix_g, w_in, q_a_norm_g, w_q_b, kv_a_norm_g, w_kv_b, q_norm_g, k_norm_g, w_o_mla, ssm_lambda_re, ssm_lambda_im, ssm_log_dt, ssm_b_re, ssm_b_im, ssm_c_re, ssm_c_im, ssm_d, w_glu, b_glu, w_o_ssm, mem_norm_g, w_mem_kv, xq_norm_g, xk_norm_g, w_o_cross, b_gate, w_out, norm_ffn_g, w_up, conv_w, conv_b, w_down):
    bsz, seq, _ = x.shape
    t = bsz * seq
    assert seq % max(TM_FRONT, TM_MIX, TQ, S5_CHUNK) == 0 and t % ROPE_ROWS == 0

    inv_freq = ROPE_THETA ** (-jnp.arange(0, D_ROPE, 2, dtype=F32) / D_ROPE)
    invf = jnp.concatenate([jnp.zeros((D_NOPE,), F32), inv_freq, inv_freq,
                            jnp.zeros((HEAD_PAD - D_QK,), F32)]).reshape(1, 1, HEAD_PAD)
    tables = _rope_tables(positions.reshape(t, 1), invf)

    ptab, bb_re, bb_im = _s5_prep(ssm_lambda_re, ssm_lambda_im, ssm_log_dt, ssm_b_re, ssm_b_im)
    w = _stacked_weights(dict(
        norm_mix_g=norm_mix_g, w_in=w_in, q_a_norm_g=q_a_norm_g, w_q_b=w_q_b,
        kv_a_norm_g=kv_a_norm_g, w_kv_b=w_kv_b, q_norm_g=q_norm_g, k_norm_g=k_norm_g,
        w_o_mla=w_o_mla, w_o_ssm=w_o_ssm, w_o_cross=w_o_cross, b_gate=b_gate, w_out=w_out,
        norm_ffn_g=norm_ffn_g, w_up=w_up, conv_w=conv_w, conv_b=conv_b, w_down=w_down,
        xq_norm_g=xq_norm_g, mem_norm_g=mem_norm_g, w_mem_kv=w_mem_kv, xk_norm_g=xk_norm_g,
        bb_re=bb_re, bb_im=bb_im, ssm_c_re=ssm_c_re, ssm_c_im=ssm_c_im, ssm_d=ssm_d,
        w_glu=w_glu, b_glu=b_glu))

    x2 = x.reshape(t, D_MODEL)
    for i in range(DEPTH):
        kx, vx = _memkv(mem, w["gmem"], w["wmem"], w["gxk"], i)
        q, k, vt, u, yc = _front(x2, tables, kx, vx, w, i, bsz, seq)
        ya = _attention(q, k, vt).reshape(t, MLA_WIDTH)
        yb = _s5(_segment_major(u.reshape(bsz, seq, SSM_WIDTH), bsz, seq), w["bmat"], w["cre"],
                 w["cim"], ptab, w["ssm_d"], w["wglu"], w["bglu"], i)
        yb = _time_major(yb, bsz, seq).reshape(t, SSM_WIDTH)
        x2 = _mix(x2, ya, yb, yc, w, i, seq)
    return x2.reshape(bsz, seq, D_MODEL)
```

```python
import functools
import math

import jax
import jax.numpy as jnp
from jax import lax
from jax.experimental import pallas as pl
from jax.experimental.pallas import tpu as pltpu

F32 = jnp.float32
BF16 = jnp.bfloat16

D_MODEL = 1024
DEPTH = 2
MEM_LEN = 256
EPS = 1e-6
MLA_HEADS = 8
Q_LORA = 384
KV_LORA = 256
D_NOPE = 64
D_ROPE = 32
D_QK = D_NOPE + D_ROPE
D_V = 64
MLA_WIDTH = MLA_HEADS * D_V
ROPE_THETA = 10000.0
SSM_GROUPS = 32
SSM_GROUP_CH = 16
SSM_WIDTH = SSM_GROUPS * SSM_GROUP_CH
SSM_STATE = 64
SSM_NS = SSM_GROUPS * SSM_STATE
X_HEADS = 4
X_HEAD_DIM = 128
X_WIDTH = X_HEADS * X_HEAD_DIM
N_BRANCH = 3
D_FF = 2816
CONV_WIDTH = 3

LANES = 128
SUBLANES = 8
HEAD_PAD = LANES

TM_FRONT = 1024
TM_MIX = 512
FFN_CHUNK = 256
TQ = 512
TK = 256
ATTN_HEADS = 4
V_ROWS = D_V + 16
S5_SEG = 32
S5_CHUNK = SUBLANES * S5_SEG
S5_COLS = 512
S5_HALVES = 2
ROPE_ROWS = 512

VMEM_LIMIT = 56 * 1024 * 1024


def _layer_spec(arr, layer):
    nd = arr.ndim - 1
    return pl.BlockSpec((None,) + arr.shape[1:], lambda *_: (layer,) + (0,) * nd,
                        pipeline_mode=pl.Buffered(1))


def _rms(x, g):
    return x * lax.rsqrt(jnp.mean(x * x, axis=-1, keepdims=True) + EPS) * g


def _dot(a, b):
    return jnp.dot(a, b, preferred_element_type=F32)


def _dot_nt(a, b):
    return lax.dot_general(a, b, (((1,), (1,)), ((), ())), preferred_element_type=F32)


def _params(sem, limit=VMEM_LIMIT):
    return pltpu.CompilerParams(dimension_semantics=sem, vmem_limit_bytes=limit)


def _rope_kernel(pos_ref, invf_ref, cos_ref, sin_ref, nsin_ref):
    ang = pos_ref[...].astype(F32) * invf_ref[...]
    s = jnp.sin(ang)
    cos_ref[...] = jnp.cos(ang)
    sin_ref[...] = s
    nsin_ref[...] = -s


def _rope_tables(positions, inv_freq):
    t = positions.size
    half = D_ROPE // 2
    per_row = LANES // half
    rows = t // per_row
    pos = jnp.repeat(positions.reshape(rows, per_row), half, axis=1)
    invf = jnp.tile(inv_freq, per_row).reshape(1, 1, LANES)
    out = jax.ShapeDtypeStruct((rows, LANES), F32)
    spec = pl.BlockSpec((ROPE_ROWS, LANES), lambda i: (i, 0))
    cos, sin, nsin = pl.pallas_call(
        _rope_kernel,
        grid=(rows // ROPE_ROWS,),
        in_specs=[spec, _layer_spec(invf, 0)],
        out_specs=[spec, spec, spec],
        out_shape=[out, out, out],
        compiler_params=_params(("arbitrary",)),
        name="rope_tables",
    )(pos, invf)
    cos, sin, nsin = (a.reshape(t, half) for a in (cos, sin, nsin))
    cc = jnp.concatenate([jnp.ones((t, D_NOPE), F32), cos, cos,
                          jnp.zeros((t, HEAD_PAD - D_QK), F32)], axis=1)
    ss = jnp.concatenate([jnp.zeros((t, D_NOPE), F32), nsin, sin,
                          jnp.zeros((t, HEAD_PAD - D_QK), F32)], axis=1)
    return cc, ss


def _memkv_kernel(mem_ref, g_ref, w_ref, gk_ref, k_ref, v_ref):
    m = _rms(mem_ref[0], g_ref[...]).astype(BF16)
    kv = _dot(m, w_ref[...])
    for h in range(X_HEADS):
        sl = slice(h * X_HEAD_DIM, (h + 1) * X_HEAD_DIM)
        k_ref[0, :, sl] = _rms(kv[:, sl], gk_ref[...]).astype(BF16)
    v_ref[0] = kv[:, X_WIDTH:].astype(BF16)


def _memkv(mem, g, w, gk, layer):
    b = mem.shape[0]
    out = jax.ShapeDtypeStruct((b, MEM_LEN, X_WIDTH), BF16)
    spec = pl.BlockSpec((1, MEM_LEN, X_WIDTH), lambda i: (i, 0, 0))
    return pl.pallas_call(
        _memkv_kernel,
        grid=(b,),
        in_specs=[pl.BlockSpec((1, MEM_LEN, D_MODEL), lambda i: (i, 0, 0)),
                  _layer_spec(g, layer), _layer_spec(w, layer), _layer_spec(gk, layer)],
        out_specs=[spec, spec],
        out_shape=[out, out],
        compiler_params=_params(("arbitrary",)),
        name="mem_kv",
    )(mem, g, w, gk)


def _front_kernel(x_ref, cc_ref, ss_ref, kx_ref, vx_ref, gmix_ref, win_ref,
                  gqa_ref, wq_ref, gkva_ref, wk_ref, wvt_ref, gq_ref, gqs_ref, gk_ref, gks_ref,
                  gxq_ref, q_ref, k_ref, vt_ref, u_ref, yc_ref):
    tm = x_ref.shape[0]
    h = _rms(x_ref[...], gmix_ref[...]).astype(BF16)
    p = _dot(h, win_ref[...])
    o_ckv = Q_LORA
    o_kr = o_ckv + KV_LORA
    o_krs = o_kr + HEAD_PAD
    o_u = o_krs + HEAD_PAD
    o_xq = o_u + SSM_WIDTH
    cq = p[:, :o_ckv]
    ckv = p[:, o_ckv:o_kr]
    kr = p[:, o_kr:o_krs]
    krs = p[:, o_krs:o_u]
    u_ref[...] = p[:, o_u:o_xq].astype(BF16)
    xq = p[:, o_xq:]

    cqn = _rms(cq, gqa_ref[...]).astype(BF16)
    ckvn = _rms(ckv, gkva_ref[...]).astype(BF16)

    vt = _dot_nt(wvt_ref[...], ckvn)
    ones = jnp.ones((V_ROWS - D_V, TK), BF16)
    for hd in range(MLA_HEADS):
        for c in range(tm // TK):
            vt_ref[0, hd, c, :D_V, :] = vt[hd * D_V:(hd + 1) * D_V, c * TK:(c + 1) * TK].astype(BF16)
            vt_ref[0, hd, c, D_V:, :] = ones

    cc = cc_ref[...]
    ss = ss_ref[...]
    nq = MLA_HEADS * HEAD_PAD

    def inv_rms(a):
        return lax.rsqrt(jnp.sum(a * a, axis=-1, keepdims=True) * (1.0 / D_QK) + EPS)

    qa = _dot(cqn, wq_ref[...])
    gc = gq_ref[...] * cc
    gs = gqs_ref[...] * ss
    scale = D_QK ** -0.5 * math.log2(math.e)
    for hd in range(MLA_HEADS):
        a = qa[:, hd * HEAD_PAD:(hd + 1) * HEAD_PAD]
        a_s = qa[:, nq + hd * HEAD_PAD:nq + (hd + 1) * HEAD_PAD]
        q_ref[0, hd] = ((inv_rms(a) * scale) * (a * gc + a_s * gs)).astype(BF16)

    ka = _dot(ckvn, wk_ref[...])
    gc = gk_ref[...] * cc
    rot = krs * (gks_ref[...] * ss)
    for hd in range(MLA_HEADS):
        a = ka[:, hd * HEAD_PAD:(hd + 1) * HEAD_PAD] + kr
        k_ref[0, hd] = (inv_rms(a) * (a * gc + rot)).astype(BF16)

    xscale = X_HEAD_DIM ** -0.5
    for hd in range(X_HEADS):
        sl = slice(hd * X_HEAD_DIM, (hd + 1) * X_HEAD_DIM)
        qx = (_rms(xq[:, sl], gxq_ref[...]) * xscale).astype(BF16)
        s = _dot_nt(qx, kx_ref[0, :, sl])
        pm = jnp.exp(s - jnp.max(s, axis=-1, keepdims=True))
        l = jnp.sum(pm, axis=-1, keepdims=True)
        yc_ref[:, sl] = (_dot(pm.astype(BF16), vx_ref[0, :, sl]) / l).astype(BF16)


def _front(x2, tables, kx, vx, w, layer, bsz, seq):
    t = bsz * seq
    tm = TM_FRONT
    nb = seq // tm
    row = lambda c: pl.BlockSpec((tm, c), lambda i: (i, 0))
    hspec = pl.BlockSpec((1, MLA_HEADS, tm, HEAD_PAD), lambda i: (i // nb, 0, i % nb, 0))
    mspec = pl.BlockSpec((1, MEM_LEN, X_WIDTH), lambda i: (i // nb, 0, 0))
    consts = [w["gmix"], w["win"], w["gqa"], w["wq"], w["gkva"], w["wk"], w["wvt"],
              w["gq"], w["gqs"], w["gk"], w["gks"], w["gxq"]]
    hshape = jax.ShapeDtypeStruct((bsz, MLA_HEADS, seq, HEAD_PAD), BF16)
    tshape = jax.ShapeDtypeStruct((t, SSM_WIDTH), BF16)
    vtshape = jax.ShapeDtypeStruct((bsz, MLA_HEADS, seq // TK, V_ROWS, TK), BF16)
    vtspec = pl.BlockSpec((1, MLA_HEADS, tm // TK, V_ROWS, TK), lambda i: (i // nb, 0, i % nb, 0, 0))
    return pl.pallas_call(
        _front_kernel,
        grid=(t // tm,),
        in_specs=[row(D_MODEL), row(HEAD_PAD), row(HEAD_PAD), mspec, mspec]
                 + [_layer_spec(c, layer) for c in consts],
        out_specs=[hspec, hspec, vtspec, row(SSM_WIDTH), row(X_WIDTH)],
        out_shape=[hshape, hshape, vtshape, tshape, tshape],
        compiler_params=_params(("arbitrary",)),
        name="front",
    )(x2, *tables, kx, vx, *consts)


def _attn_kernel(q_ref, k_ref, vt_ref, o_ref, st_scr, m_scr, acc_scr):
    qi = pl.program_id(2)
    m_scr[...] = jnp.full(m_scr.shape, -jnp.inf, F32)
    acc_scr[...] = jnp.zeros(acc_scr.shape, F32)

    def scores(kb, slot, hh):
        r = pl.multiple_of(kb * TK, TK)
        st_scr[slot, hh] = _dot_nt(k_ref[0, hh, pl.ds(r, TK), :], q_ref[0, hh])

    def consume(kb, slot, hh, masked):
        st = st_scr[slot, hh]
        if masked:
            key = kb * TK + lax.broadcasted_iota(jnp.int32, st.shape, 0)
            qry = qi * TQ + lax.broadcasted_iota(jnp.int32, st.shape, 1)
            st = jnp.where(key <= qry, st, -jnp.inf)
        m_old = m_scr[hh]
        m_new = jnp.maximum(m_old, jnp.max(st, axis=0, keepdims=True))
        pm = jnp.exp2(st - m_new).astype(BF16)
        alpha = jnp.exp2(m_old - m_new)
        acc_scr[hh] = alpha * acc_scr[hh] + _dot(vt_ref[0, hh, kb], pm)
        m_scr[hh] = m_new

    steps = TQ // TK
    for hh in range(ATTN_HEADS):
        scores(0, 0, hh)

    def body(j, c):
        for s in range(steps):
            kb = j * steps + s
            for hh in range(ATTN_HEADS):
                scores(kb + 1, (s + 1) % 2, hh)
                consume(kb, s % 2, hh, False)
        return c

    lax.fori_loop(0, qi, body, 0)
    for s in range(steps):
        kb = qi * steps + s
        for hh in range(ATTN_HEADS):
            if s + 1 < steps:
                scores(kb + 1, (s + 1) % 2, hh)
            consume(kb, s % 2, hh, True)
    outs = []
    for hh in range(ATTN_HEADS):
        a = acc_scr[hh]
        outs.append(a[:D_V] / a[D_V:D_V + 1])
    o_ref[0] = jnp.concatenate(outs, axis=0).T.astype(BF16)


def _attention(q, k, vt):
    bsz, nh, seq, _ = q.shape
    assert TQ % TK == 0 and (TQ // TK) % 2 == 0
    nh_blk = ATTN_HEADS
    return pl.pallas_call(
        _attn_kernel,
        grid=(bsz, nh // nh_blk, seq // TQ),
        in_specs=[pl.BlockSpec((1, nh_blk, TQ, HEAD_PAD), lambda b, hg, i: (b, hg, i, 0)),
                  pl.BlockSpec((1, nh_blk, seq, HEAD_PAD), lambda b, hg, i: (b, hg, 0, 0)),
                  pl.BlockSpec((1, nh_blk, seq // TK, V_ROWS, TK), lambda b, hg, i: (b, hg, 0, 0, 0))],
        out_specs=pl.BlockSpec((1, TQ, nh_blk * D_V), lambda b, hg, i: (b, i, hg)),
        out_shape=jax.ShapeDtypeStruct((bsz, seq, nh * D_V), BF16),
        scratch_shapes=[pltpu.VMEM((2, nh_blk, TK, TQ), F32), pltpu.VMEM((nh_blk, 1, TQ), F32),
                        pltpu.VMEM((nh_blk, V_ROWS, TQ), F32)],
        compiler_params=_params(("arbitrary", "arbitrary", "arbitrary")),
        name="mla_attention",
    )(q, k, vt)


def _s5_prep_kernel(lr_ref, li_ref, ldt_ref, lrc_ref, lic_ref, ldtc_ref, bre_ref, bim_ref,
                    ptab_ref, bbre_ref, bbim_ref):
    ns = SSM_NS
    dt = jnp.exp(ldt_ref[0])
    lr = lr_ref[0] * dt
    li = li_ref[0] * dt
    k = (lax.broadcasted_iota(jnp.int32, (S5_SEG, 1), 0) + 1).astype(F32)
    mag = jnp.exp(k * lr)
    ang = k * li
    p_re = mag * jnp.cos(ang)
    p_im = mag * jnp.sin(ang)
    for r in range(S5_SEG):
        rows = slice(r * SUBLANES, (r + 1) * SUBLANES)
        ptab_ref[0, rows, :ns] = jnp.broadcast_to(p_re[r:r + 1], (SUBLANES, ns))
        ptab_ref[0, rows, ns:] = jnp.broadcast_to(p_im[r:r + 1], (SUBLANES, ns))
    dtc = jnp.exp(ldtc_ref[0])
    lrc = lrc_ref[0]
    lic = lic_ref[0]
    magc = jnp.exp(lrc * dtc)
    e_re = magc * jnp.cos(lic * dtc) - 1.0
    e_im = magc * jnp.sin(lic * dtc)
    den = lrc * lrc + lic * lic
    f_re = (e_re * lrc + e_im * lic) / den
    f_im = (e_im * lrc - e_re * lic) / den
    bre = bre_ref[0]
    bim = bim_ref[0]
    bbre_ref[0] = f_re * bre - f_im * bim
    bbim_ref[0] = f_re * bim + f_im * bre


def _s5_prep(lam_re, lam_im, log_dt, b_re, b_im):
    depth = lam_re.shape[0]
    ns = SSM_NS
    ldt = jnp.broadcast_to(log_dt[:, :, None], lam_re.shape)
    rowv = lambda a: a.reshape(depth, 1, ns)
    colv = lambda a: a.reshape(depth, ns, 1)
    bcol = lambda a: a.reshape(depth, ns, SSM_GROUP_CH)
    spec = lambda s: pl.BlockSpec((1,) + s, lambda i: (i, 0, 0))
    return pl.pallas_call(
        _s5_prep_kernel,
        grid=(depth,),
        in_specs=[spec((1, ns))] * 3 + [spec((ns, 1))] * 3 + [spec((ns, SSM_GROUP_CH))] * 2,
        out_specs=[spec((S5_CHUNK, 2 * ns)),
                   spec((ns, SSM_GROUP_CH)), spec((ns, SSM_GROUP_CH))],
        out_shape=[jax.ShapeDtypeStruct((depth, S5_CHUNK, 2 * ns), F32),
                   jax.ShapeDtypeStruct((depth, ns, SSM_GROUP_CH), F32),
                   jax.ShapeDtypeStruct((depth, ns, SSM_GROUP_CH), F32)],
        compiler_params=_params(("arbitrary",)),
        name="s5_prep",
    )(rowv(lam_re), rowv(lam_im), rowv(ldt), colv(lam_re), colv(lam_im), colv(ldt),
      bcol(b_re), bcol(b_im))


def _s5_kernel(u_ref, bmat_ref, cre_ref, cim_ref, ptab_ref, d_ref, wglu_ref, bglu_ref,
               o_ref, s_scr, carry_scr, c_scr):
    ns = SSM_NS
    cw = S5_COLS
    nb = u_ref.shape[0]
    hw = SSM_WIDTH // S5_HALVES
    hs = ns // S5_HALVES

    @pl.when(pl.program_id(0) == 0)
    def _():
        carry_scr[...] = jnp.zeros(carry_scr.shape, F32)

    for b in range(nb):
        for hf in range(S5_HALVES):
            bu = _dot(u_ref[b, :, hf * hw:(hf + 1) * hw], bmat_ref[hf])
            s_scr[b, :, hf * hs:(hf + 1) * hs] = bu[:, :hs]
            s_scr[b, :, ns + hf * hs:ns + (hf + 1) * hs] = bu[:, hs:]

    cols = lambda cg: (slice(cg * cw, (cg + 1) * cw), slice(ns + cg * cw, ns + (cg + 1) * cw))
    ncg = ns // cw

    for b in range(nb):
        z = {}
        for i in range(S5_SEG):
            rows = slice(i * SUBLANES, (i + 1) * SUBLANES)
            for cg in range(ncg):
                re, im = cols(cg)
                if i == 0:
                    z[cg] = (s_scr[b, rows, re], s_scr[b, rows, im])
                    continue
                a_re = ptab_ref[:SUBLANES, re]
                a_im = ptab_ref[:SUBLANES, im]
                z_re, z_im = z[cg]
                n_re = a_re * z_re - a_im * z_im + s_scr[b, rows, re]
                n_im = a_re * z_im + a_im * z_re + s_scr[b, rows, im]
                s_scr[b, rows, re] = n_re
                s_scr[b, rows, im] = n_im
                z[cg] = (n_re, n_im)

        c = {}
        for cg in range(ncg):
            re, im = cols(cg)
            e_re, e_im = z[cg]
            p_re = ptab_ref[S5_CHUNK - 1:S5_CHUNK, re]
            p_im = ptab_ref[S5_CHUNK - 1:S5_CHUNK, im]
            cur_re = carry_scr[b, :, re]
            cur_im = carry_scr[b, :, im]
            for j in range(SUBLANES):
                c_scr[b, j:j + 1, re] = cur_re
                c_scr[b, j:j + 1, im] = cur_im
                n_re = e_re[j:j + 1, :] + p_re * cur_re - p_im * cur_im
                n_im = e_im[j:j + 1, :] + p_re * cur_im + p_im * cur_re
                cur_re, cur_im = n_re, n_im
            carry_scr[b, :, re] = cur_re
            carry_scr[b, :, im] = cur_im
            c[cg] = (c_scr[b, :, re], c_scr[b, :, im])

        for i in range(S5_SEG):
            rows = slice(i * SUBLANES, (i + 1) * SUBLANES)
            for cg in range(ncg):
                re, im = cols(cg)
                c_re, c_im = c[cg]
                q_re = ptab_ref[rows, re]
                q_im = ptab_ref[rows, im]
                s_scr[b, rows, re] = s_scr[b, rows, re] + q_re * c_re - q_im * c_im
                s_scr[b, rows, im] = s_scr[b, rows, im] + q_re * c_im + q_im * c_re

    for b in range(nb):
        ys = []
        for hf in range(S5_HALVES):
            s_re = s_scr[b, :, hf * hs:(hf + 1) * hs].astype(BF16)
            s_im = s_scr[b, :, ns + hf * hs:ns + (hf + 1) * hs].astype(BF16)
            ys.append(_dot(s_re, cre_ref[hf]) - _dot(s_im, cim_ref[hf]))
        y = jnp.concatenate(ys, axis=1) + d_ref[...] * u_ref[b].astype(F32)
        y = jax.nn.gelu(y)
        zz = _dot(y.astype(BF16), wglu_ref[...]) + bglu_ref[...]
        o_ref[b] = (y * jax.nn.sigmoid(zz)).astype(BF16)


def _s5(u_perm, bmat, cre, cim, ptab, d, wglu, bglu, layer):
    bsz, seq, _ = u_perm.shape
    ns = SSM_NS
    blk = pl.BlockSpec((bsz, S5_CHUNK, SSM_WIDTH), lambda j: (0, j, 0))
    consts = [bmat, cre, cim, ptab, d, wglu, bglu]
    return pl.pallas_call(
        _s5_kernel,
        grid=(seq // S5_CHUNK,),
        in_specs=[blk] + [_layer_spec(c, layer) for c in consts],
        out_specs=blk,
        out_shape=jax.ShapeDtypeStruct((bsz, seq, SSM_WIDTH), BF16),
        scratch_shapes=[pltpu.VMEM((bsz, S5_CHUNK, 2 * ns), F32), pltpu.VMEM((bsz, 1, 2 * ns), F32),
                        pltpu.VMEM((bsz, SUBLANES, 2 * ns), F32)],
        compiler_params=_params(("arbitrary",)),
        name="s5_scan",
    )(u_perm, *consts)


def _mix_kernel(x_ref, ya_ref, yb_ref, yc_ref, gmix_ref, wg_ref, bg_ref, woa_ref, wob_ref,
                woc_ref, wout_ref, g_ref, wu_ref, cw_ref, cb_ref, wd_ref, o_ref, carry_g, carry_v,
                *, blocks_per_seq):
    tm = x_ref.shape[0]

    @pl.when(pl.program_id(0) % blocks_per_seq == 0)
    def _():
        carry_g[...] = jnp.zeros(carry_g.shape, F32)
        carry_v[...] = jnp.zeros(carry_v.shape, F32)

    x = x_ref[...]
    h = _rms(x, gmix_ref[...]).astype(BF16)
    merged = None
    for br, (y_ref, wo_ref) in enumerate(((ya_ref, woa_ref), (yb_ref, wob_ref), (yc_ref, woc_ref))):
        sl = slice(br * D_MODEL, (br + 1) * D_MODEL)
        gate = jax.nn.sigmoid(_dot(h, wg_ref[:, sl]) + bg_ref[:, sl])
        term = gate * _dot(y_ref[...], wo_ref[...])
        merged = term if merged is None else merged + term
    x = x + _dot(merged.astype(BF16), wout_ref[...])

    h2 = _rms(x, g_ref[...]).astype(BF16)
    row = lax.broadcasted_iota(jnp.int32, (tm, 1), 0)

    def up_proj(c):
        lo = c * FFN_CHUNK
        return (_dot(h2, wu_ref[:, lo:lo + FFN_CHUNK]),
                _dot(h2, wu_ref[:, D_FF + lo:D_FF + lo + FFN_CHUNK]))

    def conv(cs, up, off, carry):
        ws = slice(off + cs.start, off + cs.stop)
        prev = carry[:, cs]
        carry[:, cs] = up[tm - SUBLANES:, :]
        p1 = prev[SUBLANES - 1:SUBLANES, :]
        p2 = prev[SUBLANES - 2:SUBLANES - 1, :]
        m1 = jnp.where(row == 0, p1, pltpu.roll(up, 1, 0))
        m2 = jnp.where(row == 0, p2, jnp.where(row == 1, p1, pltpu.roll(up, 2, 0)))
        return cw_ref[0:1, ws] * m2 + cw_ref[1:2, ws] * m1 + cw_ref[2:3, ws] * up + cb_ref[:, ws]

    nchunk = D_FF // FFN_CHUNK
    acc = jnp.zeros((tm, D_MODEL), F32)
    ahead = 2
    ups = [up_proj(c) for c in range(ahead)]
    for c in range(nchunk):
        if c + ahead < nchunk:
            ups.append(up_proj(c + ahead))
        cs = slice(c * FFN_CHUNK, (c + 1) * FFN_CHUNK)
        gv = conv(cs, ups[c][0], 0, carry_g)
        vv = conv(cs, ups[c][1], D_FF, carry_v)
        act = (gv * jax.nn.sigmoid(gv) * vv).astype(BF16)
        acc = acc + _dot(act, wd_ref[cs, :])
    o_ref[...] = x + acc


def _mix(x2, ya, yb, yc, w, layer, seq):
    t = x2.shape[0]
    tm = TM_MIX
    row = lambda c: pl.BlockSpec((tm, c), lambda i: (i, 0))
    consts = [w["gmix"], w["wg"], w["bg"], w["woa"], w["wob"], w["woc"], w["wout"],
              w["gffn"], w["wu"], w["cw"], w["cb"], w["wd"]]
    return pl.pallas_call(
        functools.partial(_mix_kernel, blocks_per_seq=seq // tm),
        grid=(t // tm,),
        in_specs=[row(D_MODEL), row(MLA_WIDTH), row(SSM_WIDTH), row(X_WIDTH)]
                 + [_layer_spec(c, layer) for c in consts],
        out_specs=row(D_MODEL),
        out_shape=jax.ShapeDtypeStruct((t, D_MODEL), F32),
        scratch_shapes=[pltpu.VMEM((SUBLANES, D_FF), F32), pltpu.VMEM((SUBLANES, D_FF), F32)],
        compiler_params=_params(("arbitrary",)),
        name="mix",
    )(x2, ya, yb, yc, *consts)


def _block_diag(blocks):
    g, r, c = blocks.shape
    eye = jnp.eye(g, dtype=bool)
    full = jnp.where(eye[:, None, :, None], blocks[:, :, None, :], jnp.zeros((), blocks.dtype))
    return full.reshape(g * r, g * c)


def _pad_lanes(a, lo, width=HEAD_PAD):
    pad = [(0, 0)] * (a.ndim - 1) + [(lo, width - lo - a.shape[-1])]
    return jnp.pad(a, pad)


def _swap_rope(a):
    half = D_ROPE // 2
    lo = a[..., D_NOPE:D_NOPE + half]
    hi = a[..., D_NOPE + half:D_QK]
    return jnp.concatenate([jnp.zeros_like(a[..., :D_NOPE]), hi, lo,
                            jnp.zeros_like(a[..., D_QK:])], axis=-1)


def _stacked_weights(p):
    depth = p["w_in"].shape[0]
    w_in = p["w_in"]
    o = 0
    parts = {}
    for name, width in (("cq", Q_LORA), ("ckv", KV_LORA), ("kr", D_ROPE), ("u", SSM_WIDTH),
                        ("xq", X_WIDTH), ("g", N_BRANCH * D_MODEL)):
        parts[name] = w_in[:, :, o:o + width]
        o += width
    kr = _pad_lanes(parts["kr"], D_NOPE)
    win = jnp.concatenate([parts["cq"], parts["ckv"], kr, _swap_rope(kr),
                           parts["u"], parts["xq"]], axis=2).astype(BF16)
    wq = _pad_lanes(p["w_q_b"].reshape(depth, Q_LORA, MLA_HEADS, D_QK), 0)
    wq = jnp.concatenate([wq.reshape(depth, Q_LORA, -1),
                          _swap_rope(wq).reshape(depth, Q_LORA, -1)], axis=2)
    wkv = p["w_kv_b"].reshape(depth, KV_LORA, MLA_HEADS, D_NOPE + D_V)
    wk = _pad_lanes(wkv[..., :D_NOPE], 0).reshape(depth, KV_LORA, -1)
    wvt = wkv[..., D_NOPE:].reshape(depth, KV_LORA, MLA_WIDTH).transpose(0, 2, 1)
    row = lambda a: a.reshape(depth, 1, -1)
    gq = _pad_lanes(row(p["q_norm_g"]), 0)
    gk = _pad_lanes(row(p["k_norm_g"]), 0)
    shp = (depth, S5_HALVES, SSM_GROUPS // S5_HALVES, SSM_STATE, SSM_GROUP_CH)
    cshp = (depth, S5_HALVES, SSM_GROUPS // S5_HALVES, SSM_GROUP_CH, SSM_STATE)
    bd = jax.vmap(jax.vmap(_block_diag))
    return {
        "gmix": row(p["norm_mix_g"]),
        "win": win,
        "gqa": row(p["q_a_norm_g"]),
        "wq": wq.astype(BF16),
        "gkva": row(p["kv_a_norm_g"]),
        "wk": wk.astype(BF16),
        "wvt": wvt.astype(BF16),
        "gq": gq,
        "gqs": _swap_rope(gq),
        "gk": gk,
        "gks": _swap_rope(gk),
        "gxq": row(p["xq_norm_g"]),
        "wg": parts["g"].astype(BF16),
        "bg": row(p["b_gate"]),
        "woa": p["w_o_mla"].astype(BF16),
        "wob": p["w_o_ssm"].astype(BF16),
        "woc": p["w_o_cross"].astype(BF16),
        "wout": p["w_out"].astype(BF16),
        "gffn": row(p["norm_ffn_g"]),
        "wu": p["w_up"].astype(BF16),
        "cw": p["conv_w"],
        "cb": row(p["conv_b"]),
        "wd": p["w_down"].astype(BF16),
        "gmem": row(p["mem_norm_g"]),
        "wmem": p["w_mem_kv"].astype(BF16),
        "gxk": row(p["xk_norm_g"]),
        "bmat": jnp.concatenate(
            [bd(p["bb_re"].reshape(shp).transpose(0, 1, 2, 4, 3)),
             bd(p["bb_im"].reshape(shp).transpose(0, 1, 2, 4, 3))], axis=3).astype(BF16),
        "cre": bd(p["ssm_c_re"].reshape(cshp).transpose(0, 1, 2, 4, 3)).astype(BF16),
        "cim": bd(p["ssm_c_im"].reshape(cshp).transpose(0, 1, 2, 4, 3)).astype(BF16),
        "ssm_d": row(p["ssm_d"]),
        "wglu": p["w_glu"].astype(BF16),
        "bglu": row(p["b_glu"]),
    }


def _segment_major(a, bsz, seq):
    c = a.shape[-1]
    a = a.reshape(bsz, seq // S5_CHUNK, SUBLANES, S5_SEG, c)
    return a.transpose(0, 1, 3, 2, 4).reshape(bsz, seq, c)


def _time_major(a, bsz, seq):
    c = a.shape[-1]
    a = a.reshape(bsz, seq // S5_CHUNK, S5_SEG, SUBLANES, c)
    return a.transpose(0, 1, 3, 2, 4).reshape(bsz, seq, c)


def kernel(x, mem, positions, norm_mix_g, w_in, q_a_norm_g, w_q_b, kv_a_norm_g, w_kv_b, q_norm_g, k_norm_g, w_o_mla, ssm_lambda_re, ssm_lambda_im, ssm_log_dt, ssm_b_re, ssm_b_im, ssm_c_re, ssm_c_im, ssm_d, w_glu, b_glu, w_o_ssm, mem_norm_g, w_mem_kv, xq_norm_g, xk_norm_g, w_o_cross, b_gate, w_out, norm_ffn_g, w_up, conv_w, conv_b, w_down):
    bsz, seq, _ = x.shape
    t = bsz * seq
    assert seq % max(TM_FRONT, TM_MIX, TQ, S5_CHUNK) == 0 and t % (ROPE_ROWS * LANES // (D_ROPE // 2)) == 0

    inv_freq = ROPE_THETA ** (-jnp.arange(0, D_ROPE, 2, dtype=F32) / D_ROPE)
    tables = _rope_tables(positions, inv_freq)

    ptab, bb_re, bb_im = _s5_prep(ssm_lambda_re, ssm_lambda_im, ssm_log_dt, ssm_b_re, ssm_b_im)
    w = _stacked_weights(dict(
        norm_mix_g=norm_mix_g, w_in=w_in, q_a_norm_g=q_a_norm_g, w_q_b=w_q_b,
        kv_a_norm_g=kv_a_norm_g, w_kv_b=w_kv_b, q_norm_g=q_norm_g, k_norm_g=k_norm_g,
        w_o_mla=w_o_mla, w_o_ssm=w_o_ssm, w_o_cross=w_o_cross, b_gate=b_gate, w_out=w_out,
        norm_ffn_g=norm_ffn_g, w_up=w_up, conv_w=conv_w, conv_b=conv_b, w_down=w_down,
        xq_norm_g=xq_norm_g, mem_norm_g=mem_norm_g, w_mem_kv=w_mem_kv, xk_norm_g=xk_norm_g,
        bb_re=bb_re, bb_im=bb_im, ssm_c_re=ssm_c_re, ssm_c_im=ssm_c_im, ssm_d=ssm_d,
        w_glu=w_glu, b_glu=b_glu))

    x2 = x.reshape(t, D_MODEL)
    for i in range(DEPTH):
        kx, vx = _memkv(mem, w["gmem"], w["wmem"], w["gxk"], i)
        q, k, vt, u, yc = _front(x2, tables, kx, vx, w, i, bsz, seq)
        ya = _attention(q, k, vt).reshape(t, MLA_WIDTH)
        yb = _s5(_segment_major(u.reshape(bsz, seq, SSM_WIDTH), bsz, seq), w["bmat"], w["cre"],
                 w["cim"], ptab, w["ssm_d"], w["wglu"], w["bglu"], i)
        yb = _time_major(yb, bsz, seq).reshape(t, SSM_WIDTH)
        x2 = _mix(x2, ya, yb, yc, w, i, seq)
    return x2.reshape(bsz, seq, D_MODEL)
```

```python
import functools
import math

import jax
import jax.numpy as jnp
from jax import lax
from jax.experimental import pallas as pl
from jax.experimental.pallas import tpu as pltpu

F32 = jnp.float32
BF16 = jnp.bfloat16

D_MODEL = 1024
DEPTH = 2
MEM_LEN = 256
EPS = 1e-6
MLA_HEADS = 8
Q_LORA = 384
KV_LORA = 256
D_NOPE = 64
D_ROPE = 32
D_QK = D_NOPE + D_ROPE
D_V = 64
MLA_WIDTH = MLA_HEADS * D_V
ROPE_THETA = 10000.0
SSM_GROUPS = 32
SSM_GROUP_CH = 16
SSM_WIDTH = SSM_GROUPS * SSM_GROUP_CH
SSM_STATE = 64
SSM_NS = SSM_GROUPS * SSM_STATE
X_HEADS = 4
X_HEAD_DIM = 128
X_WIDTH = X_HEADS * X_HEAD_DIM
N_BRANCH = 3
D_FF = 2816
CONV_WIDTH = 3

LANES = 128
SUBLANES = 8
HEAD_PAD = LANES
FRONT_COLS = Q_LORA + KV_LORA + 2 * HEAD_PAD + SSM_WIDTH + X_WIDTH

TM_FRONT = 1024
TM_MIX = 512
FFN_CHUNK = 256
DOWN_GROUP = 4
TQ = 512
TK = 256
ATTN_HEADS = 4
V_ROWS = D_V + 16
S5_SEG = 32
S5_CHUNK = SUBLANES * S5_SEG
S5_COLS = 512
S5_HALVES = 2
ROPE_ROWS = 512

VMEM_LIMIT = 56 * 1024 * 1024


def _layer_spec(arr, layer, width=None, block=0):
    nd = arr.ndim - 1
    shape = arr.shape[1:] if width is None else arr.shape[1:-1] + (width,)
    return pl.BlockSpec((None,) + shape, lambda *_: (layer,) + (0,) * (nd - 1) + (block,),
                        pipeline_mode=pl.Buffered(1))


def _rms(x, g):
    return x * lax.rsqrt(jnp.mean(x * x, axis=-1, keepdims=True) + EPS) * g


def _dot(a, b):
    return jnp.dot(a, b, preferred_element_type=F32)


def _dot_nt(a, b):
    return lax.dot_general(a, b, (((1,), (1,)), ((), ())), preferred_element_type=F32)


def _params(sem, limit=VMEM_LIMIT):
    return pltpu.CompilerParams(dimension_semantics=sem, vmem_limit_bytes=limit)


def _rope_kernel(pos_ref, invf_ref, cos_ref, sin_ref, nsin_ref):
    ang = pos_ref[...].astype(F32) * invf_ref[...]
    s = jnp.sin(ang)
    cos_ref[...] = jnp.cos(ang)
    sin_ref[...] = s
    nsin_ref[...] = -s


def _rope_tables(positions, inv_freq):
    t = positions.size
    half = D_ROPE // 2
    per_row = LANES // half
    rows = t // per_row
    pos = jnp.repeat(positions.reshape(rows, per_row), half, axis=1)
    invf = jnp.tile(inv_freq, per_row).reshape(1, 1, LANES)
    out = jax.ShapeDtypeStruct((rows, LANES), F32)
    spec = pl.BlockSpec((ROPE_ROWS, LANES), lambda i: (i, 0))
    cos, sin, nsin = pl.pallas_call(
        _rope_kernel,
        grid=(rows // ROPE_ROWS,),
        in_specs=[spec, _layer_spec(invf, 0)],
        out_specs=[spec, spec, spec],
        out_shape=[out, out, out],
        compiler_params=_params(("arbitrary",)),
        name="rope_tables",
    )(pos, invf)
    cos, sin, nsin = (a.reshape(t, half) for a in (cos, sin, nsin))
    cc = jnp.concatenate([jnp.ones((t, D_NOPE), F32), cos, cos,
                          jnp.zeros((t, HEAD_PAD - D_QK), F32)], axis=1)
    ss = jnp.concatenate([jnp.zeros((t, D_NOPE), F32), nsin, sin,
                          jnp.zeros((t, HEAD_PAD - D_QK), F32)], axis=1)
    return cc, ss


def _memkv_kernel(mem_ref, g_ref, w_ref, gk_ref, k_ref, v_ref):
    m = _rms(mem_ref[0], g_ref[...]).astype(BF16)
    kv = _dot(m, w_ref[...])
    for h in range(X_HEADS):
        sl = slice(h * X_HEAD_DIM, (h + 1) * X_HEAD_DIM)
        k_ref[0, :, sl] = _rms(kv[:, sl], gk_ref[...]).astype(BF16)
    v_ref[0] = kv[:, X_WIDTH:].astype(BF16)


def _memkv(mem, g, w, gk, layer):
    b = mem.shape[0]
    out = jax.ShapeDtypeStruct((b, MEM_LEN, X_WIDTH), BF16)
    spec = pl.BlockSpec((1, MEM_LEN, X_WIDTH), lambda i: (i, 0, 0))
    return pl.pallas_call(
        _memkv_kernel,
        grid=(b,),
        in_specs=[pl.BlockSpec((1, MEM_LEN, D_MODEL), lambda i: (i, 0, 0)),
                  _layer_spec(g, layer), _layer_spec(w, layer), _layer_spec(gk, layer)],
        out_specs=[spec, spec],
        out_shape=[out, out],
        compiler_params=_params(("arbitrary",)),
        name="mem_kv",
    )(mem, g, w, gk)


def _front_kernel(x_ref, cc_ref, ss_ref, kx_ref, vx_ref, gmix_ref, win_ref,
                  gqa_ref, wq_ref, gkva_ref, wk_ref, wvt_ref, gq_ref, gqs_ref, gk_ref, gks_ref,
                  gxq_ref, q_ref, k_ref, vt_ref, u_ref, yc_ref):
    tm = x_ref.shape[0]
    h = _rms(x_ref[...], gmix_ref[...]).astype(BF16)
    p = _dot(h, win_ref[...])
    o_ckv = Q_LORA
    o_kr = o_ckv + KV_LORA
    o_krs = o_kr + HEAD_PAD
    o_u = o_krs + HEAD_PAD
    o_xq = o_u + SSM_WIDTH
    cq = p[:, :o_ckv]
    ckv = p[:, o_ckv:o_kr]
    kr = p[:, o_kr:o_krs]
    krs = p[:, o_krs:o_u]
    u_ref[...] = p[:, o_u:o_xq].astype(BF16)
    xq = p[:, o_xq:]

    cqn = _rms(cq, gqa_ref[...]).astype(BF16)
    ckvn = _rms(ckv, gkva_ref[...]).astype(BF16)

    vt = _dot_nt(wvt_ref[...], ckvn)
    ones = jnp.ones((V_ROWS - D_V, TK), BF16)
    for hd in range(MLA_HEADS):
        for c in range(tm // TK):
            vt_ref[0, hd, c, :D_V, :] = vt[hd * D_V:(hd + 1) * D_V, c * TK:(c + 1) * TK].astype(BF16)
            vt_ref[0, hd, c, D_V:, :] = ones

    cc = cc_ref[...]
    ss = ss_ref[...]
    nq = MLA_HEADS * HEAD_PAD

    def inv_rms(a):
        return lax.rsqrt(jnp.sum(a * a, axis=-1, keepdims=True) * (1.0 / D_QK) + EPS)

    qa = _dot(cqn, wq_ref[...])
    gc = gq_ref[...] * cc
    gs = gqs_ref[...] * ss
    scale = D_QK ** -0.5 * math.log2(math.e)
    for hd in range(MLA_HEADS):
        a = qa[:, hd * HEAD_PAD:(hd + 1) * HEAD_PAD]
        a_s = qa[:, nq + hd * HEAD_PAD:nq + (hd + 1) * HEAD_PAD]
        q_ref[0, hd] = ((inv_rms(a) * scale) * (a * gc + a_s * gs)).astype(BF16)

    ka = _dot(ckvn, wk_ref[...])
    gc = gk_ref[...] * cc
    rot = krs * (gks_ref[...] * ss)
    for hd in range(MLA_HEADS):
        a = ka[:, hd * HEAD_PAD:(hd + 1) * HEAD_PAD] + kr
        k_ref[0, hd] = (inv_rms(a) * (a * gc + rot)).astype(BF16)

    xscale = X_HEAD_DIM ** -0.5
    for hd in range(X_HEADS):
        sl = slice(hd * X_HEAD_DIM, (hd + 1) * X_HEAD_DIM)
        qx = (_rms(xq[:, sl], gxq_ref[...]) * xscale).astype(BF16)
        s = _dot_nt(qx, kx_ref[0, :, sl])
        pm = jnp.exp(s - jnp.max(s, axis=-1, keepdims=True))
        l = jnp.sum(pm, axis=-1, keepdims=True)
        yc_ref[:, sl] = (_dot(pm.astype(BF16), vx_ref[0, :, sl]) / l).astype(BF16)


def _front(x2, tables, kx, vx, w, layer, bsz, seq):
    t = bsz * seq
    tm = TM_FRONT
    nb = seq // tm
    row = lambda c: pl.BlockSpec((tm, c), lambda i: (i, 0))
    hspec = pl.BlockSpec((1, MLA_HEADS, tm, HEAD_PAD), lambda i: (i // nb, 0, i % nb, 0))
    mspec = pl.BlockSpec((1, MEM_LEN, X_WIDTH), lambda i: (i // nb, 0, 0))
    consts = [w["gmix"], w["wcat"], w["gqa"], w["wq"], w["gkva"], w["wk"], w["wvt"],
              w["gq"], w["gqs"], w["gk"], w["gks"], w["gxq"]]
    fw = FRONT_COLS
    assert w["wcat"].shape[2] % fw == 0
    win_spec = _layer_spec(w["wcat"], layer, fw, w["wcat"].shape[2] // fw - 1)
    hshape = jax.ShapeDtypeStruct((bsz, MLA_HEADS, seq, HEAD_PAD), BF16)
    tshape = jax.ShapeDtypeStruct((t, SSM_WIDTH), BF16)
    vtshape = jax.ShapeDtypeStruct((bsz, MLA_HEADS, seq // TK, V_ROWS, TK), BF16)
    vtspec = pl.BlockSpec((1, MLA_HEADS, tm // TK, V_ROWS, TK), lambda i: (i // nb, 0, i % nb, 0, 0))
    return pl.pallas_call(
        _front_kernel,
        grid=(t // tm,),
        in_specs=[row(D_MODEL), row(HEAD_PAD), row(HEAD_PAD), mspec, mspec]
                 + [win_spec if c is w["wcat"] else _layer_spec(c, layer) for c in consts],
        out_specs=[hspec, hspec, vtspec, row(SSM_WIDTH), row(X_WIDTH)],
        out_shape=[hshape, hshape, vtshape, tshape, tshape],
        compiler_params=_params(("arbitrary",)),
        name="front",
    )(x2, *tables, kx, vx, *consts)


def _attn_kernel(q_ref, k_ref, vt_ref, o_ref, st_scr, m_scr, acc_scr):
    qi = pl.program_id(2)
    m_scr[...] = jnp.full(m_scr.shape, -jnp.inf, F32)
    acc_scr[...] = jnp.zeros(acc_scr.shape, F32)

    per = TQ // TK

    def scores(kb, slot, hh):
        r = pl.multiple_of(kb * TQ, TQ)
        st_scr[slot, hh] = _dot_nt(k_ref[0, hh, pl.ds(r, TQ), :], q_ref[0, hh])

    def consume(kb, slot, hh, masked):
        st = st_scr[slot, hh]
        if masked:
            key = lax.broadcasted_iota(jnp.int32, st.shape, 0)
            qry = lax.broadcasted_iota(jnp.int32, st.shape, 1)
            st = jnp.where(key <= qry, st, -jnp.inf)
        m_old = m_scr[hh]
        m_new = jnp.maximum(m_old, jnp.max(st, axis=0, keepdims=True))
        pm = jnp.exp2(st - m_new).astype(BF16)
        alpha = jnp.exp2(m_old - m_new)
        vt = jnp.concatenate([vt_ref[0, hh, kb * per + i] for i in range(per)], axis=1)
        acc_scr[hh] = alpha * acc_scr[hh] + _dot(vt, pm)
        m_scr[hh] = m_new

    def step(kb, slot, masked, prefetch):
        for hh in range(ATTN_HEADS):
            if prefetch:
                scores(kb + 1, 1 - slot, hh)
            consume(kb, slot, hh, masked)

    for hh in range(ATTN_HEADS):
        scores(0, 0, hh)

    def body(j, c):
        step(2 * j, 0, False, True)
        step(2 * j + 1, 1, False, True)
        return c

    lax.fori_loop(0, qi // 2, body, 0)

    @pl.when(qi % 2 == 1)
    def _():
        step(qi - 1, 0, False, True)
        step(qi, 1, True, False)

    @pl.when(qi % 2 == 0)
    def _():
        step(qi, 0, True, False)

    outs = []
    for hh in range(ATTN_HEADS):
        a = acc_scr[hh]
        outs.append(a[:D_V] / a[D_V:D_V + 1])
    o_ref[0] = jnp.concatenate(outs, axis=0).T.astype(BF16)


def _attention(q, k, vt):
    bsz, nh, seq, _ = q.shape
    assert TQ % TK == 0
    nh_blk = ATTN_HEADS
    return pl.pallas_call(
        _attn_kernel,
        grid=(bsz, nh // nh_blk, seq // TQ),
        in_specs=[pl.BlockSpec((1, nh_blk, TQ, HEAD_PAD), lambda b, hg, i: (b, hg, i, 0)),
                  pl.BlockSpec((1, nh_blk, seq, HEAD_PAD), lambda b, hg, i: (b, hg, 0, 0)),
                  pl.BlockSpec((1, nh_blk, seq // TK, V_ROWS, TK), lambda b, hg, i: (b, hg, 0, 0, 0))],
        out_specs=pl.BlockSpec((1, TQ, nh_blk * D_V), lambda b, hg, i: (b, i, hg)),
        out_shape=jax.ShapeDtypeStruct((bsz, seq, nh * D_V), BF16),
        scratch_shapes=[pltpu.VMEM((2, nh_blk, TQ, TQ), F32), pltpu.VMEM((nh_blk, 1, TQ), F32),
                        pltpu.VMEM((nh_blk, V_ROWS, TQ), F32)],
        compiler_params=_params(("arbitrary", "arbitrary", "arbitrary")),
        name="mla_attention",
    )(q, k, vt)


def _s5_prep_kernel(lr_ref, li_ref, ldt_ref, lrc_ref, lic_ref, ldtc_ref, bre_ref, bim_ref,
                    ptab_ref, bbre_ref, bbim_ref):
    ns = SSM_NS
    dt = jnp.exp(ldt_ref[0])
    lr = lr_ref[0] * dt
    li = li_ref[0] * dt
    k = (lax.broadcasted_iota(jnp.int32, (S5_SEG, 1), 0) + 1).astype(F32)
    mag = jnp.exp(k * lr)
    ang = k * li
    p_re = mag * jnp.cos(ang)
    p_im = mag * jnp.sin(ang)
    for r in range(S5_SEG):
        rows = slice(r * SUBLANES, (r + 1) * SUBLANES)
        ptab_ref[0, rows, :ns] = jnp.broadcast_to(p_re[r:r + 1], (SUBLANES, ns))
        ptab_ref[0, rows, ns:] = jnp.broadcast_to(p_im[r:r + 1], (SUBLANES, ns))
    dtc = jnp.exp(ldtc_ref[0])
    lrc = lrc_ref[0]
    lic = lic_ref[0]
    magc = jnp.exp(lrc * dtc)
    e_re = magc * jnp.cos(lic * dtc) - 1.0
    e_im = magc * jnp.sin(lic * dtc)
    den = lrc * lrc + lic * lic
    f_re = (e_re * lrc + e_im * lic) / den
    f_im = (e_im * lrc - e_re * lic) / den
    bre = bre_ref[0]
    bim = bim_ref[0]
    bbre_ref[0] = f_re * bre - f_im * bim
    bbim_ref[0] = f_re * bim + f_im * bre


def _s5_prep(lam_re, lam_im, log_dt, b_re, b_im):
    depth = lam_re.shape[0]
    ns = SSM_NS
    ldt = jnp.broadcast_to(log_dt[:, :, None], lam_re.shape)
    rowv = lambda a: a.reshape(depth, 1, ns)
    colv = lambda a: a.reshape(depth, ns, 1)
    bcol = lambda a: a.reshape(depth, ns, SSM_GROUP_CH)
    spec = lambda s: pl.BlockSpec((1,) + s, lambda i: (i, 0, 0))
    return pl.pallas_call(
        _s5_prep_kernel,
        grid=(depth,),
        in_specs=[spec((1, ns))] * 3 + [spec((ns, 1))] * 3 + [spec((ns, SSM_GROUP_CH))] * 2,
        out_specs=[spec((S5_CHUNK, 2 * ns)),
                   spec((ns, SSM_GROUP_CH)), spec((ns, SSM_GROUP_CH))],
        out_shape=[jax.ShapeDtypeStruct((depth, S5_CHUNK, 2 * ns), F32),
                   jax.ShapeDtypeStruct((depth, ns, SSM_GROUP_CH), F32),
                   jax.ShapeDtypeStruct((depth, ns, SSM_GROUP_CH), F32)],
        compiler_params=_params(("arbitrary",)),
        name="s5_prep",
    )(rowv(lam_re), rowv(lam_im), rowv(ldt), colv(lam_re), colv(lam_im), colv(ldt),
      bcol(b_re), bcol(b_im))


def _s5_kernel(u_ref, bmat_ref, cre_ref, cim_ref, ptab_ref, d_ref, wglu_ref, bglu_ref,
               o_ref, s_scr, carry_scr, c_scr):
    ns = SSM_NS
    cw = S5_COLS
    nb = u_ref.shape[0]
    hw = SSM_WIDTH // S5_HALVES
    hs = ns // S5_HALVES

    @pl.when(pl.program_id(0) == 0)
    def _():
        carry_scr[...] = jnp.zeros(carry_scr.shape, F32)

    for b in range(nb):
        for hf in range(S5_HALVES):
            bu = _dot(u_ref[b, :, hf * hw:(hf + 1) * hw], bmat_ref[hf])
            s_scr[b, :, hf * hs:(hf + 1) * hs] = bu[:, :hs]
            s_scr[b, :, ns + hf * hs:ns + (hf + 1) * hs] = bu[:, hs:]

    cols = lambda cg: (slice(cg * cw, (cg + 1) * cw), slice(ns + cg * cw, ns + (cg + 1) * cw))
    ncg = ns // cw

    for b in range(nb):
        z = {}
        for i in range(S5_SEG):
            rows = slice(i * SUBLANES, (i + 1) * SUBLANES)
            for cg in range(ncg):
                re, im = cols(cg)
                if i == 0:
                    z[cg] = (s_scr[b, rows, re], s_scr[b, rows, im])
                    continue
                a_re = ptab_ref[:SUBLANES, re]
                a_im = ptab_ref[:SUBLANES, im]
                z_re, z_im = z[cg]
                n_re = a_re * z_re - a_im * z_im + s_scr[b, rows, re]
                n_im = a_re * z_im + a_im * z_re + s_scr[b, rows, im]
                s_scr[b, rows, re] = n_re
                s_scr[b, rows, im] = n_im
                z[cg] = (n_re, n_im)

        c = {}
        for cg in range(ncg):
            re, im = cols(cg)
            e_re, e_im = z[cg]
            p_re = ptab_ref[S5_CHUNK - 1:S5_CHUNK, re]
            p_im = ptab_ref[S5_CHUNK - 1:S5_CHUNK, im]
            cur_re = carry_scr[b, :, re]
            cur_im = carry_scr[b, :, im]
            for j in range(SUBLANES):
                c_scr[b, j:j + 1, re] = cur_re
                c_scr[b, j:j + 1, im] = cur_im
                n_re = e_re[j:j + 1, :] + p_re * cur_re - p_im * cur_im
                n_im = e_im[j:j + 1, :] + p_re * cur_im + p_im * cur_re
                cur_re, cur_im = n_re, n_im
            carry_scr[b, :, re] = cur_re
            carry_scr[b, :, im] = cur_im
            c[cg] = (c_scr[b, :, re], c_scr[b, :, im])

        for i in range(S5_SEG):
            rows = slice(i * SUBLANES, (i + 1) * SUBLANES)
            for cg in range(ncg):
                re, im = cols(cg)
                c_re, c_im = c[cg]
                q_re = ptab_ref[rows, re]
                q_im = ptab_ref[rows, im]
                s_scr[b, rows, re] = s_scr[b, rows, re] + q_re * c_re - q_im * c_im
                s_scr[b, rows, im] = s_scr[b, rows, im] + q_re * c_im + q_im * c_re

    for b in range(nb):
        ys = []
        for hf in range(S5_HALVES):
            s_re = s_scr[b, :, hf * hs:(hf + 1) * hs].astype(BF16)
            s_im = s_scr[b, :, ns + hf * hs:ns + (hf + 1) * hs].astype(BF16)
            ys.append(_dot(s_re, cre_ref[hf]) - _dot(s_im, cim_ref[hf]))
        y = jnp.concatenate(ys, axis=1) + d_ref[...] * u_ref[b].astype(F32)
        y = jax.nn.gelu(y)
        zz = _dot(y.astype(BF16), wglu_ref[...]) + bglu_ref[...]
        o_ref[b] = (y * jax.nn.sigmoid(zz)).astype(BF16)


def _s5(u_perm, bmat, cre, cim, ptab, d, wglu, bglu, layer):
    bsz, seq, _ = u_perm.shape
    ns = SSM_NS
    blk = pl.BlockSpec((bsz, S5_CHUNK, SSM_WIDTH), lambda j: (0, j, 0))
    consts = [bmat, cre, cim, ptab, d, wglu, bglu]
    return pl.pallas_call(
        _s5_kernel,
        grid=(seq // S5_CHUNK,),
        in_specs=[blk] + [_layer_spec(c, layer) for c in consts],
        out_specs=blk,
        out_shape=jax.ShapeDtypeStruct((bsz, seq, SSM_WIDTH), BF16),
        scratch_shapes=[pltpu.VMEM((bsz, S5_CHUNK, 2 * ns), F32), pltpu.VMEM((bsz, 1, 2 * ns), F32),
                        pltpu.VMEM((bsz, SUBLANES, 2 * ns), F32)],
        compiler_params=_params(("arbitrary",)),
        name="s5_scan",
    )(u_perm, *consts)


def _mix_kernel(x_ref, ya_ref, yb_ref, yc_ref, gmix_ref, wg_ref, bg_ref, woa_ref, wob_ref,
                woc_ref, wout_ref, g_ref, wu_ref, cw_ref, cb_ref, wd_ref, o_ref, carry_g, carry_v,
                *, blocks_per_seq):
    tm = x_ref.shape[0]

    @pl.when(pl.program_id(0) % blocks_per_seq == 0)
    def _():
        carry_g[...] = jnp.zeros(carry_g.shape, F32)
        carry_v[...] = jnp.zeros(carry_v.shape, F32)

    x = x_ref[...]
    h = _rms(x, gmix_ref[...]).astype(BF16)
    merged = None
    for br, (y_ref, wo_ref) in enumerate(((ya_ref, woa_ref), (yb_ref, wob_ref), (yc_ref, woc_ref))):
        sl = slice(br * D_MODEL, (br + 1) * D_MODEL)
        gate = jax.nn.sigmoid(_dot(h, wg_ref[:, sl]) + bg_ref[:, sl])
        term = gate * _dot(y_ref[...], wo_ref[...])
        merged = term if merged is None else merged + term
    x = x + _dot(merged.astype(BF16), wout_ref[...])

    h2 = _rms(x, g_ref[...]).astype(BF16)
    row = lax.broadcasted_iota(jnp.int32, (tm, 1), 0)

    def up_proj(c):
        lo = c * FFN_CHUNK
        return (_dot(h2, wu_ref[:, lo:lo + FFN_CHUNK]),
                _dot(h2, wu_ref[:, D_FF + lo:D_FF + lo + FFN_CHUNK]))

    def conv(cs, up, off, carry):
        ws = slice(off + cs.start, off + cs.stop)
        prev = carry[:, cs]
        carry[:, cs] = up[tm - SUBLANES:, :]
        p1 = prev[SUBLANES - 1:SUBLANES, :]
        p2 = prev[SUBLANES - 2:SUBLANES - 1, :]
        m1 = jnp.where(row == 0, p1, pltpu.roll(up, 1, 0))
        m2 = jnp.where(row == 0, p2, jnp.where(row == 1, p1, pltpu.roll(up, 2, 0)))
        return cw_ref[0:1, ws] * m2 + cw_ref[1:2, ws] * m1 + cw_ref[2:3, ws] * up + cb_ref[:, ws]

    nchunk = D_FF // FFN_CHUNK
    acc = jnp.zeros((tm, D_MODEL), F32)
    ahead = 2
    ups = [up_proj(c) for c in range(ahead)]
    acts = []
    for c in range(nchunk):
        if c + ahead < nchunk:
            ups.append(up_proj(c + ahead))
        cs = slice(c * FFN_CHUNK, (c + 1) * FFN_CHUNK)
        gv = conv(cs, ups[c][0], 0, carry_g)
        vv = conv(cs, ups[c][1], D_FF, carry_v)
        acts.append((gv * jax.nn.sigmoid(gv) * vv).astype(BF16))
        if len(acts) == DOWN_GROUP or c + 1 == nchunk:
            lo = (c + 1 - len(acts)) * FFN_CHUNK
            act = acts[0] if len(acts) == 1 else jnp.concatenate(acts, axis=1)
            acc = acc + _dot(act, wd_ref[lo:(c + 1) * FFN_CHUNK, :])
            acts = []
    o_ref[...] = x + acc


def _mix(x2, ya, yb, yc, w, layer, seq):
    t = x2.shape[0]
    tm = TM_MIX
    row = lambda c: pl.BlockSpec((tm, c), lambda i: (i, 0))
    gspec = _layer_spec(w["wcat"], layer, N_BRANCH * D_MODEL, 0)
    consts = [w["gmix"], w["wcat"], w["bg"], w["woa"], w["wob"], w["woc"], w["wout"],
              w["gffn"], w["wu"], w["cw"], w["cb"], w["wd"]]
    return pl.pallas_call(
        functools.partial(_mix_kernel, blocks_per_seq=seq // tm),
        grid=(t // tm,),
        in_specs=[row(D_MODEL), row(MLA_WIDTH), row(SSM_WIDTH), row(X_WIDTH)]
                 + [gspec if c is w["wcat"] else _layer_spec(c, layer) for c in consts],
        out_specs=row(D_MODEL),
        out_shape=jax.ShapeDtypeStruct((t, D_MODEL), F32),
        scratch_shapes=[pltpu.VMEM((SUBLANES, D_FF), F32), pltpu.VMEM((SUBLANES, D_FF), F32)],
        compiler_params=_params(("arbitrary",)),
        name="mix",
    )(x2, ya, yb, yc, *consts)


def _block_diag(blocks):
    g, r, c = blocks.shape
    eye = jnp.eye(g, dtype=bool)
    full = jnp.where(eye[:, None, :, None], blocks[:, :, None, :], jnp.zeros((), blocks.dtype))
    return full.reshape(g * r, g * c)


def _pad_lanes(a, lo, width=HEAD_PAD):
    pad = [(0, 0)] * (a.ndim - 1) + [(lo, width - lo - a.shape[-1])]
    return jnp.pad(a, pad)


def _swap_rope(a):
    half = D_ROPE // 2
    lo = a[..., D_NOPE:D_NOPE + half]
    hi = a[..., D_NOPE + half:D_QK]
    return jnp.concatenate([jnp.zeros_like(a[..., :D_NOPE]), hi, lo,
                            jnp.zeros_like(a[..., D_QK:])], axis=-1)


def _stacked_weights(p):
    depth = p["w_in"].shape[0]
    w_in = p["w_in"]
    o = 0
    parts = {}
    for name, width in (("cq", Q_LORA), ("ckv", KV_LORA), ("kr", D_ROPE), ("u", SSM_WIDTH),
                        ("xq", X_WIDTH), ("g", N_BRANCH * D_MODEL)):
        parts[name] = w_in[:, :, o:o + width]
        o += width
    kr = _pad_lanes(parts["kr"], D_NOPE)
    front_cols = [parts["cq"], parts["ckv"], kr, _swap_rope(kr), parts["u"], parts["xq"]]
    fw = sum(c.shape[2] for c in front_cols)
    gw = parts["g"].shape[2]
    pad = jnp.zeros((depth, D_MODEL, -gw % fw), w_in.dtype)
    wcat = jnp.concatenate([parts["g"], pad] + front_cols, axis=2).astype(BF16)
    wq = _pad_lanes(p["w_q_b"].reshape(depth, Q_LORA, MLA_HEADS, D_QK), 0)
    wq = jnp.concatenate([wq.reshape(depth, Q_LORA, -1),
                          _swap_rope(wq).reshape(depth, Q_LORA, -1)], axis=2)
    wkv = p["w_kv_b"].reshape(depth, KV_LORA, MLA_HEADS, D_NOPE + D_V)
    wk = _pad_lanes(wkv[..., :D_NOPE], 0).reshape(depth, KV_LORA, -1)
    wvt = wkv[..., D_NOPE:].reshape(depth, KV_LORA, MLA_WIDTH).transpose(0, 2, 1)
    row = lambda a: a.reshape(depth, 1, -1)
    gq = _pad_lanes(row(p["q_norm_g"]), 0)
    gk = _pad_lanes(row(p["k_norm_g"]), 0)
    shp = (depth, S5_HALVES, SSM_GROUPS // S5_HALVES, SSM_STATE, SSM_GROUP_CH)
    cshp = (depth, S5_HALVES, SSM_GROUPS // S5_HALVES, SSM_GROUP_CH, SSM_STATE)
    bd = jax.vmap(jax.vmap(_block_diag))
    return {
        "gmix": row(p["norm_mix_g"]),
        "wcat": wcat,
        "gqa": row(p["q_a_norm_g"]),
        "wq": wq.astype(BF16),
        "gkva": row(p["kv_a_norm_g"]),
        "wk": wk.astype(BF16),
        "wvt": wvt.astype(BF16),
        "gq": gq,
        "gqs": _swap_rope(gq),
        "gk": gk,
        "gks": _swap_rope(gk),
        "gxq": row(p["xq_norm_g"]),
        "bg": row(p["b_gate"]),
        "woa": p["w_o_mla"].astype(BF16),
        "wob": p["w_o_ssm"].astype(BF16),
        "woc": p["w_o_cross"].astype(BF16),
        "wout": p["w_out"].astype(BF16),
        "gffn": row(p["norm_ffn_g"]),
        "wu": p["w_up"].astype(BF16),
        "cw": p["conv_w"],
        "cb": row(p["conv_b"]),
        "wd": p["w_down"].astype(BF16),
        "gmem": row(p["mem_norm_g"]),
        "wmem": p["w_mem_kv"].astype(BF16),
        "gxk": row(p["xk_norm_g"]),
        "bmat": jnp.concatenate(
            [bd(p["bb_re"].reshape(shp).transpose(0, 1, 2, 4, 3)),
             bd(p["bb_im"].reshape(shp).transpose(0, 1, 2, 4, 3))], axis=3).astype(BF16),
        "cre": bd(p["ssm_c_re"].reshape(cshp).transpose(0, 1, 2, 4, 3)).astype(BF16),
        "cim": bd(p["ssm_c_im"].reshape(cshp).transpose(0, 1, 2, 4, 3)).astype(BF16),
        "ssm_d": row(p["ssm_d"]),
        "wglu": p["w_glu"].astype(BF16),
        "bglu": row(p["b_glu"]),
    }


def _segment_major(a, bsz, seq):
    c = a.shape[-1]
    a = a.reshape(bsz, seq // S5_CHUNK, SUBLANES, S5_SEG, c)
    return a.transpose(0, 1, 3, 2, 4).reshape(bsz, seq, c)


def _time_major(a, bsz, seq):
    c = a.shape[-1]
    a = a.reshape(bsz, seq // S5_CHUNK, S5_SEG, SUBLANES, c)
    return a.transpose(0, 1, 3, 2, 4).reshape(bsz, seq, c)


def kernel(x, mem, positions, norm_mix_g, w_in, q_a_norm_g, w_q_b, kv_a_norm_g, w_kv_b, q_norm_g, k_norm_g, w_o_mla, ssm_lambda_re, ssm_lambda_im, ssm_log_dt, ssm_b_re, ssm_b_im, ssm_c_re, ssm_c_im, ssm_d, w_glu, b_glu, w_o_ssm, mem_norm_g, w_mem_kv, xq_norm_g, xk_norm_g, w_o_cross, b_gate, w_out, norm_ffn_g, w_up, conv_w, conv_b, w_down):
    bsz, seq, _ = x.shape
    t = bsz * seq
    assert seq % max(TM_FRONT, TM_MIX, TQ, S5_CHUNK) == 0 and t % (ROPE_ROWS * LANES // (D_ROPE // 2)) == 0

    inv_freq = ROPE_THETA ** (-jnp.arange(0, D_ROPE, 2, dtype=F32) / D_ROPE)
    tables = _rope_tables(positions, inv_freq)

    ptab, bb_re, bb_im = _s5_prep(ssm_lambda_re, ssm_lambda_im, ssm_log_dt, ssm_b_re, ssm_b_im)
    w = _stacked_weights(dict(
        norm_mix_g=norm_mix_g, w_in=w_in, q_a_norm_g=q_a_norm_g, w_q_b=w_q_b,
        kv_a_norm_g=kv_a_norm_g, w_kv_b=w_kv_b, q_norm_g=q_norm_g, k_norm_g=k_norm_g,
        w_o_mla=w_o_mla, w_o_ssm=w_o_ssm, w_o_cross=w_o_cross, b_gate=b_gate, w_out=w_out,
        norm_ffn_g=norm_ffn_g, w_up=w_up, conv_w=conv_w, conv_b=conv_b, w_down=w_down,
        xq_norm_g=xq_norm_g, mem_norm_g=mem_norm_g, w_mem_kv=w_mem_kv, xk_norm_g=xk_norm_g,
        bb_re=bb_re, bb_im=bb_im, ssm_c_re=ssm_c_re, ssm_c_im=ssm_c_im, ssm_d=ssm_d,
        w_glu=w_glu, b_glu=b_glu))

    x2 = x.reshape(t, D_MODEL)
    for i in range(DEPTH):
        kx, vx = _memkv(mem, w["gmem"], w["wmem"], w["gxk"], i)
        q, k, vt, u, yc = _front(x2, tables, kx, vx, w, i, bsz, seq)
        ya = _attention(q, k, vt).reshape(t, MLA_WIDTH)
        yb = _s5(_segment_major(u.reshape(bsz, seq, SSM_WIDTH), bsz, seq), w["bmat"], w["cre"],
                 w["cim"], ptab, w["ssm_d"], w["wglu"], w["bglu"], i)
        yb = _time_major(yb, bsz, seq).reshape(t, SSM_WIDTH)
        x2 = _mix(x2, ya, yb, yc, w, i, seq)
    return x2.reshape(bsz, seq, D_MODEL)
```

```python
import functools
import math

import jax
import jax.numpy as jnp
from jax import lax
from jax.experimental import pallas as pl
from jax.experimental.pallas import tpu as pltpu

F32 = jnp.float32
BF16 = jnp.bfloat16

D_MODEL = 1024
DEPTH = 2
MEM_LEN = 256
EPS = 1e-6
MLA_HEADS = 8
Q_LORA = 384
KV_LORA = 256
D_NOPE = 64
D_ROPE = 32
D_QK = D_NOPE + D_ROPE
D_V = 64
MLA_WIDTH = MLA_HEADS * D_V
ROPE_THETA = 10000.0
SSM_GROUPS = 32
SSM_GROUP_CH = 16
SSM_WIDTH = SSM_GROUPS * SSM_GROUP_CH
SSM_STATE = 64
SSM_NS = SSM_GROUPS * SSM_STATE
X_HEADS = 4
X_HEAD_DIM = 128
X_WIDTH = X_HEADS * X_HEAD_DIM
N_BRANCH = 3
D_FF = 2816
CONV_WIDTH = 3

LANES = 128
SUBLANES = 8
HEAD_PAD = LANES

TM_FRONT = 1024
TM_MIX = 512
FFN_CHUNK = 256
DOWN_GROUP = 4
TQ = 512
TK = 256
ATTN_HEADS = 4
V_ROWS = D_V + 16
S5_SEG = 64
S5_CHUNK = SUBLANES * S5_SEG
S5_COLS = 512
S5_HALVES = 2
ROPE_ROWS = 512

VMEM_LIMIT = 56 * 1024 * 1024


def _layer_spec(arr, layer):
    nd = arr.ndim - 1
    return pl.BlockSpec((None,) + arr.shape[1:], lambda *_: (layer,) + (0,) * nd,
                        pipeline_mode=pl.Buffered(1))


def _rms(x, g):
    return x * lax.rsqrt(jnp.mean(x * x, axis=-1, keepdims=True) + EPS) * g


def _dot(a, b):
    return jnp.dot(a, b, preferred_element_type=F32)


def _dot_nt(a, b):
    return lax.dot_general(a, b, (((1,), (1,)), ((), ())), preferred_element_type=F32)


def _params(sem, limit=VMEM_LIMIT):
    return pltpu.CompilerParams(dimension_semantics=sem, vmem_limit_bytes=limit)


def _rope_kernel(pos_ref, invf_ref, cos_ref, sin_ref, nsin_ref):
    ang = pos_ref[...].astype(F32) * invf_ref[...]
    s = jnp.sin(ang)
    cos_ref[...] = jnp.cos(ang)
    sin_ref[...] = s
    nsin_ref[...] = -s


def _rope_tables(positions, inv_freq):
    t = positions.size
    half = D_ROPE // 2
    per_row = LANES // half
    rows = t // per_row
    pos = jnp.repeat(positions.reshape(rows, per_row), half, axis=1)
    invf = jnp.tile(inv_freq, per_row).reshape(1, 1, LANES)
    out = jax.ShapeDtypeStruct((rows, LANES), F32)
    spec = pl.BlockSpec((ROPE_ROWS, LANES), lambda i: (i, 0))
    cos, sin, nsin = pl.pallas_call(
        _rope_kernel,
        grid=(rows // ROPE_ROWS,),
        in_specs=[spec, _layer_spec(invf, 0)],
        out_specs=[spec, spec, spec],
        out_shape=[out, out, out],
        compiler_params=_params(("arbitrary",)),
        name="rope_tables",
    )(pos, invf)
    cos, sin, nsin = (a.reshape(t, half) for a in (cos, sin, nsin))
    cc = jnp.concatenate([jnp.ones((t, D_NOPE), F32), cos, cos,
                          jnp.zeros((t, HEAD_PAD - D_QK), F32)], axis=1)
    ss = jnp.concatenate([jnp.zeros((t, D_NOPE), F32), nsin, sin,
                          jnp.zeros((t, HEAD_PAD - D_QK), F32)], axis=1)
    return cc, ss


def _memkv_kernel(mem_ref, g_ref, w_ref, gk_ref, k_ref, v_ref):
    m = _rms(mem_ref[0], g_ref[...]).astype(BF16)
    kv = _dot(m, w_ref[...])
    for h in range(X_HEADS):
        sl = slice(h * X_HEAD_DIM, (h + 1) * X_HEAD_DIM)
        k_ref[0, :, sl] = _rms(kv[:, sl], gk_ref[...]).astype(BF16)
    v_ref[0] = kv[:, X_WIDTH:].astype(BF16)


def _memkv(mem, g, w, gk, layer):
    b = mem.shape[0]
    out = jax.ShapeDtypeStruct((b, MEM_LEN, X_WIDTH), BF16)
    spec = pl.BlockSpec((1, MEM_LEN, X_WIDTH), lambda i: (i, 0, 0))
    return pl.pallas_call(
        _memkv_kernel,
        grid=(b,),
        in_specs=[pl.BlockSpec((1, MEM_LEN, D_MODEL), lambda i: (i, 0, 0)),
                  _layer_spec(g, layer), _layer_spec(w, layer), _layer_spec(gk, layer)],
        out_specs=[spec, spec],
        out_shape=[out, out],
        compiler_params=_params(("arbitrary",)),
        name="mem_kv",
    )(mem, g, w, gk)


def _front_kernel(x_ref, cc_ref, ss_ref, kx_ref, vx_ref, gmix_ref, win_ref,
                  gqa_ref, wq_ref, gkva_ref, wk_ref, wvt_ref, gq_ref, gqs_ref, gk_ref, gks_ref,
                  gxq_ref, q_ref, k_ref, vt_ref, u_ref, yc_ref):
    tm = x_ref.shape[0]
    h = _rms(x_ref[...], gmix_ref[...]).astype(BF16)
    p = _dot(h, win_ref[...])
    o_ckv = Q_LORA
    o_kr = o_ckv + KV_LORA
    o_krs = o_kr + HEAD_PAD
    o_u = o_krs + HEAD_PAD
    o_xq = o_u + SSM_WIDTH
    cq = p[:, :o_ckv]
    ckv = p[:, o_ckv:o_kr]
    kr = p[:, o_kr:o_krs]
    krs = p[:, o_krs:o_u]
    u_ref[...] = p[:, o_u:o_xq].astype(BF16)
    xq = p[:, o_xq:]

    cqn = _rms(cq, gqa_ref[...]).astype(BF16)
    ckvn = _rms(ckv, gkva_ref[...]).astype(BF16)

    vt = _dot_nt(wvt_ref[...], ckvn)
    ones = jnp.ones((V_ROWS - D_V, TK), BF16)
    for hd in range(MLA_HEADS):
        for c in range(tm // TK):
            vt_ref[0, hd, c, :D_V, :] = vt[hd * D_V:(hd + 1) * D_V, c * TK:(c + 1) * TK].astype(BF16)
            vt_ref[0, hd, c, D_V:, :] = ones

    cc = cc_ref[...]
    ss = ss_ref[...]
    nq = MLA_HEADS * HEAD_PAD

    def inv_rms(a):
        return lax.rsqrt(jnp.sum(a * a, axis=-1, keepdims=True) * (1.0 / D_QK) + EPS)

    qa = _dot(cqn, wq_ref[...])
    gc = gq_ref[...] * cc
    gs = gqs_ref[...] * ss
    scale = D_QK ** -0.5 * math.log2(math.e)
    for hd in range(MLA_HEADS):
        a = qa[:, hd * HEAD_PAD:(hd + 1) * HEAD_PAD]
        a_s = qa[:, nq + hd * HEAD_PAD:nq + (hd + 1) * HEAD_PAD]
        q_ref[0, hd] = ((inv_rms(a) * scale) * (a * gc + a_s * gs)).astype(BF16)

    ka = _dot(ckvn, wk_ref[...])
    gc = gk_ref[...] * cc
    rot = krs * (gks_ref[...] * ss)
    for hd in range(MLA_HEADS):
        a = ka[:, hd * HEAD_PAD:(hd + 1) * HEAD_PAD] + kr
        k_ref[0, hd] = (inv_rms(a) * (a * gc + rot)).astype(BF16)

    xscale = X_HEAD_DIM ** -0.5
    for hd in range(X_HEADS):
        sl = slice(hd * X_HEAD_DIM, (hd + 1) * X_HEAD_DIM)
        qx = (_rms(xq[:, sl], gxq_ref[...]) * xscale).astype(BF16)
        s = _dot_nt(qx, kx_ref[0, :, sl])
        pm = jnp.exp(s - jnp.max(s, axis=-1, keepdims=True))
        l = jnp.sum(pm, axis=-1, keepdims=True)
        yc_ref[:, sl] = (_dot(pm.astype(BF16), vx_ref[0, :, sl]) / l).astype(BF16)


def _front(x2, tables, kx, vx, w, layer, bsz, seq):
    t = bsz * seq
    tm = TM_FRONT
    nb = seq // tm
    row = lambda c: pl.BlockSpec((tm, c), lambda i: (i, 0))
    hspec = pl.BlockSpec((1, MLA_HEADS, tm, HEAD_PAD), lambda i: (i // nb, 0, i % nb, 0))
    mspec = pl.BlockSpec((1, MEM_LEN, X_WIDTH), lambda i: (i // nb, 0, 0))
    consts = [w["gmix"], w["win"], w["gqa"], w["wq"], w["gkva"], w["wk"], w["wvt"],
              w["gq"], w["gqs"], w["gk"], w["gks"], w["gxq"]]
    hshape = jax.ShapeDtypeStruct((bsz, MLA_HEADS, seq, HEAD_PAD), BF16)
    tshape = jax.ShapeDtypeStruct((t, SSM_WIDTH), BF16)
    vtshape = jax.ShapeDtypeStruct((bsz, MLA_HEADS, seq // TK, V_ROWS, TK), BF16)
    vtspec = pl.BlockSpec((1, MLA_HEADS, tm // TK, V_ROWS, TK), lambda i: (i // nb, 0, i % nb, 0, 0))
    return pl.pallas_call(
        _front_kernel,
        grid=(t // tm,),
        in_specs=[row(D_MODEL), row(HEAD_PAD), row(HEAD_PAD), mspec, mspec]
                 + [_layer_spec(c, layer) for c in consts],
        out_specs=[hspec, hspec, vtspec, row(SSM_WIDTH), row(X_WIDTH)],
        out_shape=[hshape, hshape, vtshape, tshape, tshape],
        compiler_params=_params(("arbitrary",)),
        name="front",
    )(x2, *tables, kx, vx, *consts)


def _attn_kernel(q_ref, k_ref, vt_ref, o_ref, st_scr, m_scr, acc_scr):
    qi = pl.program_id(2)
    m_scr[...] = jnp.full(m_scr.shape, -jnp.inf, F32)
    acc_scr[...] = jnp.zeros(acc_scr.shape, F32)

    per = TQ // TK

    def scores(kb, slot, hh):
        r = pl.multiple_of(kb * TQ, TQ)
        st_scr[slot, hh] = _dot_nt(k_ref[0, hh, pl.ds(r, TQ), :], q_ref[0, hh])

    def consume(kb, slot, hh, masked):
        st = st_scr[slot, hh]
        if masked:
            key = lax.broadcasted_iota(jnp.int32, st.shape, 0)
            qry = lax.broadcasted_iota(jnp.int32, st.shape, 1)
            st = jnp.where(key <= qry, st, -jnp.inf)
        m_old = m_scr[hh]
        m_new = jnp.maximum(m_old, jnp.max(st, axis=0, keepdims=True))
        pm = jnp.exp2(st - m_new).astype(BF16)
        alpha = jnp.exp2(m_old - m_new)
        vt = jnp.concatenate([vt_ref[0, hh, kb * per + i] for i in range(per)], axis=1)
        acc_scr[hh] = alpha * acc_scr[hh] + _dot(vt, pm)
        m_scr[hh] = m_new

    def step(kb, slot, masked, prefetch):
        for hh in range(ATTN_HEADS):
            if prefetch:
                scores(kb + 1, 1 - slot, hh)
            consume(kb, slot, hh, masked)

    for hh in range(ATTN_HEADS):
        scores(0, 0, hh)

    def body(j, c):
        step(2 * j, 0, False, True)
        step(2 * j + 1, 1, False, True)
        return c

    lax.fori_loop(0, qi // 2, body, 0)

    @pl.when(qi % 2 == 1)
    def _():
        step(qi - 1, 0, False, True)
        step(qi, 1, True, False)

    @pl.when(qi % 2 == 0)
    def _():
        step(qi, 0, True, False)

    outs = []
    for hh in range(ATTN_HEADS):
        a = acc_scr[hh]
        outs.append(a[:D_V] / a[D_V:D_V + 1])
    o_ref[0] = jnp.concatenate(outs, axis=0).T.astype(BF16)


def _attention(q, k, vt):
    bsz, nh, seq, _ = q.shape
    assert TQ % TK == 0
    nh_blk = ATTN_HEADS
    return pl.pallas_call(
        _attn_kernel,
        grid=(bsz, nh // nh_blk, seq // TQ),
        in_specs=[pl.BlockSpec((1, nh_blk, TQ, HEAD_PAD), lambda b, hg, i: (b, hg, i, 0)),
                  pl.BlockSpec((1, nh_blk, seq, HEAD_PAD), lambda b, hg, i: (b, hg, 0, 0)),
                  pl.BlockSpec((1, nh_blk, seq // TK, V_ROWS, TK), lambda b, hg, i: (b, hg, 0, 0, 0))],
        out_specs=pl.BlockSpec((1, TQ, nh_blk * D_V), lambda b, hg, i: (b, i, hg)),
        out_shape=jax.ShapeDtypeStruct((bsz, seq, nh * D_V), BF16),
        scratch_shapes=[pltpu.VMEM((2, nh_blk, TQ, TQ), F32), pltpu.VMEM((nh_blk, 1, TQ), F32),
                        pltpu.VMEM((nh_blk, V_ROWS, TQ), F32)],
        compiler_params=_params(("arbitrary", "arbitrary", "arbitrary")),
        name="mla_attention",
    )(q, k, vt)


def _s5_prep_kernel(lr_ref, li_ref, ldt_ref, lrc_ref, lic_ref, ldtc_ref, bre_ref, bim_ref,
                    ptab_ref, bbre_ref, bbim_ref):
    ns = SSM_NS
    dt = jnp.exp(ldt_ref[0])
    lr = lr_ref[0] * dt
    li = li_ref[0] * dt
    k = (lax.broadcasted_iota(jnp.int32, (S5_SEG, 1), 0) + 1).astype(F32)
    mag = jnp.exp(k * lr)
    ang = k * li
    p_re = mag * jnp.cos(ang)
    p_im = mag * jnp.sin(ang)
    for r in range(S5_SEG):
        rows = slice(r * SUBLANES, (r + 1) * SUBLANES)
        ptab_ref[0, rows, :ns] = jnp.broadcast_to(p_re[r:r + 1], (SUBLANES, ns))
        ptab_ref[0, rows, ns:] = jnp.broadcast_to(p_im[r:r + 1], (SUBLANES, ns))
    dtc = jnp.exp(ldtc_ref[0])
    lrc = lrc_ref[0]
    lic = lic_ref[0]
    magc = jnp.exp(lrc * dtc)
    e_re = magc * jnp.cos(lic * dtc) - 1.0
    e_im = magc * jnp.sin(lic * dtc)
    den = lrc * lrc + lic * lic
    f_re = (e_re * lrc + e_im * lic) / den
    f_im = (e_im * lrc - e_re * lic) / den
    bre = bre_ref[0]
    bim = bim_ref[0]
    bbre_ref[0] = f_re * bre - f_im * bim
    bbim_ref[0] = f_re * bim + f_im * bre


def _s5_prep(lam_re, lam_im, log_dt, b_re, b_im):
    depth = lam_re.shape[0]
    ns = SSM_NS
    ldt = jnp.broadcast_to(log_dt[:, :, None], lam_re.shape)
    rowv = lambda a: a.reshape(depth, 1, ns)
    colv = lambda a: a.reshape(depth, ns, 1)
    bcol = lambda a: a.reshape(depth, ns, SSM_GROUP_CH)
    spec = lambda s: pl.BlockSpec((1,) + s, lambda i: (i, 0, 0))
    return pl.pallas_call(
        _s5_prep_kernel,
        grid=(depth,),
        in_specs=[spec((1, ns))] * 3 + [spec((ns, 1))] * 3 + [spec((ns, SSM_GROUP_CH))] * 2,
        out_specs=[spec((S5_CHUNK, 2 * ns)),
                   spec((ns, SSM_GROUP_CH)), spec((ns, SSM_GROUP_CH))],
        out_shape=[jax.ShapeDtypeStruct((depth, S5_CHUNK, 2 * ns), F32),
                   jax.ShapeDtypeStruct((depth, ns, SSM_GROUP_CH), F32),
                   jax.ShapeDtypeStruct((depth, ns, SSM_GROUP_CH), F32)],
        compiler_params=_params(("arbitrary",)),
        name="s5_prep",
    )(rowv(lam_re), rowv(lam_im), rowv(ldt), colv(lam_re), colv(lam_im), colv(ldt),
      bcol(b_re), bcol(b_im))


def _s5_kernel(u_ref, bmat_ref, cre_ref, cim_ref, ptab_ref, d_ref, wglu_ref, bglu_ref,
               o_ref, s_scr, carry_scr, c_scr):
    ns = SSM_NS
    cw = S5_COLS
    nb = u_ref.shape[0]
    hw = SSM_WIDTH // S5_HALVES
    hs = ns // S5_HALVES

    @pl.when(pl.program_id(0) == 0)
    def _():
        carry_scr[...] = jnp.zeros(carry_scr.shape, F32)

    for b in range(nb):
        for hf in range(S5_HALVES):
            bu = _dot(u_ref[b, :, hf * hw:(hf + 1) * hw], bmat_ref[hf])
            s_scr[b, :, hf * hs:(hf + 1) * hs] = bu[:, :hs]
            s_scr[b, :, ns + hf * hs:ns + (hf + 1) * hs] = bu[:, hs:]

    cols = lambda cg: (slice(cg * cw, (cg + 1) * cw), slice(ns + cg * cw, ns + (cg + 1) * cw))
    ncg = ns // cw

    for b in range(nb):
        z = {}
        for i in range(S5_SEG):
            rows = slice(i * SUBLANES, (i + 1) * SUBLANES)
            for cg in range(ncg):
                re, im = cols(cg)
                if i == 0:
                    z[cg] = (s_scr[b, rows, re], s_scr[b, rows, im])
                    continue
                a_re = ptab_ref[:SUBLANES, re]
                a_im = ptab_ref[:SUBLANES, im]
                z_re, z_im = z[cg]
                n_re = a_re * z_re - a_im * z_im + s_scr[b, rows, re]
                n_im = a_re * z_im + a_im * z_re + s_scr[b, rows, im]
                s_scr[b, rows, re] = n_re
                s_scr[b, rows, im] = n_im
                z[cg] = (n_re, n_im)

        c = {}
        for cg in range(ncg):
            re, im = cols(cg)
            e_re, e_im = z[cg]
            p_re = ptab_ref[S5_CHUNK - 1:S5_CHUNK, re]
            p_im = ptab_ref[S5_CHUNK - 1:S5_CHUNK, im]
            cur_re = carry_scr[b, :, re]
            cur_im = carry_scr[b, :, im]
            for j in range(SUBLANES):
                c_scr[b, j:j + 1, re] = cur_re
                c_scr[b, j:j + 1, im] = cur_im
                n_re = e_re[j:j + 1, :] + p_re * cur_re - p_im * cur_im
                n_im = e_im[j:j + 1, :] + p_re * cur_im + p_im * cur_re
                cur_re, cur_im = n_re, n_im
            carry_scr[b, :, re] = cur_re
            carry_scr[b, :, im] = cur_im
            c[cg] = (c_scr[b, :, re], c_scr[b, :, im])

        for i in range(S5_SEG):
            rows = slice(i * SUBLANES, (i + 1) * SUBLANES)
            for cg in range(ncg):
                re, im = cols(cg)
                c_re, c_im = c[cg]
                q_re = ptab_ref[rows, re]
                q_im = ptab_ref[rows, im]
                s_scr[b, rows, re] = s_scr[b, rows, re] + q_re * c_re - q_im * c_im
                s_scr[b, rows, im] = s_scr[b, rows, im] + q_re * c_im + q_im * c_re

    for b in range(nb):
        ys = []
        for hf in range(S5_HALVES):
            s_re = s_scr[b, :, hf * hs:(hf + 1) * hs].astype(BF16)
            s_im = s_scr[b, :, ns + hf * hs:ns + (hf + 1) * hs].astype(BF16)
            ys.append(_dot(s_re, cre_ref[hf]) - _dot(s_im, cim_ref[hf]))
        y = jnp.concatenate(ys, axis=1) + d_ref[...] * u_ref[b].astype(F32)
        y = jax.nn.gelu(y)
        zz = _dot(y.astype(BF16), wglu_ref[...]) + bglu_ref[...]
        o_ref[b] = (y * jax.nn.sigmoid(zz)).astype(BF16)


def _s5(u_perm, bmat, cre, cim, ptab, d, wglu, bglu, layer):
    bsz, seq, _ = u_perm.shape
    ns = SSM_NS
    blk = pl.BlockSpec((bsz, S5_CHUNK, SSM_WIDTH), lambda j: (0, j, 0))
    consts = [bmat, cre, cim, ptab, d, wglu, bglu]
    return pl.pallas_call(
        _s5_kernel,
        grid=(seq // S5_CHUNK,),
        in_specs=[blk] + [_layer_spec(c, layer) for c in consts],
        out_specs=blk,
        out_shape=jax.ShapeDtypeStruct((bsz, seq, SSM_WIDTH), BF16),
        scratch_shapes=[pltpu.VMEM((bsz, S5_CHUNK, 2 * ns), F32), pltpu.VMEM((bsz, 1, 2 * ns), F32),
                        pltpu.VMEM((bsz, SUBLANES, 2 * ns), F32)],
        compiler_params=_params(("arbitrary",)),
        name="s5_scan",
    )(u_perm, *consts)


def _mix_kernel(x_ref, ya_ref, yb_ref, yc_ref, gmix_ref, wg_ref, bg_ref, woa_ref, wob_ref,
                woc_ref, wout_ref, g_ref, wu_ref, cw_ref, cb_ref, wd_ref, o_ref, carry_g, carry_v,
                *, blocks_per_seq):
    tm = x_ref.shape[0]

    @pl.when(pl.program_id(0) % blocks_per_seq == 0)
    def _():
        carry_g[...] = jnp.zeros(carry_g.shape, F32)
        carry_v[...] = jnp.zeros(carry_v.shape, F32)

    x = x_ref[...]
    h = _rms(x, gmix_ref[...]).astype(BF16)
    merged = None
    for br, (y_ref, wo_ref) in enumerate(((ya_ref, woa_ref), (yb_ref, wob_ref), (yc_ref, woc_ref))):
        sl = slice(br * D_MODEL, (br + 1) * D_MODEL)
        gate = jax.nn.sigmoid(_dot(h, wg_ref[:, sl]) + bg_ref[:, sl])
        term = gate * _dot(y_ref[...], wo_ref[...])
        merged = term if merged is None else merged + term
    x = x + _dot(merged.astype(BF16), wout_ref[...])

    h2 = _rms(x, g_ref[...]).astype(BF16)
    row = lax.broadcasted_iota(jnp.int32, (tm, 1), 0)

    def up_proj(c):
        lo = c * FFN_CHUNK
        return (_dot(h2, wu_ref[:, lo:lo + FFN_CHUNK]),
                _dot(h2, wu_ref[:, D_FF + lo:D_FF + lo + FFN_CHUNK]))

    def conv(cs, up, off, carry):
        ws = slice(off + cs.start, off + cs.stop)
        prev = carry[:, cs]
        carry[:, cs] = up[tm - SUBLANES:, :]
        p1 = prev[SUBLANES - 1:SUBLANES, :]
        p2 = prev[SUBLANES - 2:SUBLANES - 1, :]
        m1 = jnp.where(row == 0, p1, pltpu.roll(up, 1, 0))
        m2 = jnp.where(row == 0, p2, jnp.where(row == 1, p1, pltpu.roll(up, 2, 0)))
        return cw_ref[0:1, ws] * m2 + cw_ref[1:2, ws] * m1 + cw_ref[2:3, ws] * up + cb_ref[:, ws]

    nchunk = D_FF // FFN_CHUNK
    acc = jnp.zeros((tm, D_MODEL), F32)
    ahead = 2
    ups = [up_proj(c) for c in range(ahead)]
    acts = []
    for c in range(nchunk):
        if c + ahead < nchunk:
            ups.append(up_proj(c + ahead))
        cs = slice(c * FFN_CHUNK, (c + 1) * FFN_CHUNK)
        gv = conv(cs, ups[c][0], 0, carry_g)
        vv = conv(cs, ups[c][1], D_FF, carry_v)
        acts.append((gv * jax.nn.sigmoid(gv) * vv).astype(BF16))
        if len(acts) == DOWN_GROUP or c + 1 == nchunk:
            lo = (c + 1 - len(acts)) * FFN_CHUNK
            act = acts[0] if len(acts) == 1 else jnp.concatenate(acts, axis=1)
            acc = acc + _dot(act, wd_ref[lo:(c + 1) * FFN_CHUNK, :])
            acts = []
    o_ref[...] = x + acc


def _mix(x2, ya, yb, yc, w, layer, seq):
    t = x2.shape[0]
    tm = TM_MIX
    row = lambda c: pl.BlockSpec((tm, c), lambda i: (i, 0))
    consts = [w["gmix"], w["wg"], w["bg"], w["woa"], w["wob"], w["woc"], w["wout"],
              w["gffn"], w["wu"], w["cw"], w["cb"], w["wd"]]
    return pl.pallas_call(
        functools.partial(_mix_kernel, blocks_per_seq=seq // tm),
        grid=(t // tm,),
        in_specs=[row(D_MODEL), row(MLA_WIDTH), row(SSM_WIDTH), row(X_WIDTH)]
                 + [_layer_spec(c, layer) for c in consts],
        out_specs=row(D_MODEL),
        out_shape=jax.ShapeDtypeStruct((t, D_MODEL), F32),
        scratch_shapes=[pltpu.VMEM((SUBLANES, D_FF), F32), pltpu.VMEM((SUBLANES, D_FF), F32)],
        compiler_params=_params(("arbitrary",)),
        name="mix",
    )(x2, ya, yb, yc, *consts)


def _block_diag(blocks):
    g, r, c = blocks.shape
    eye = jnp.eye(g, dtype=bool)
    full = jnp.where(eye[:, None, :, None], blocks[:, :, None, :], jnp.zeros((), blocks.dtype))
    return full.reshape(g * r, g * c)


def _pad_lanes(a, lo, width=HEAD_PAD):
    pad = [(0, 0)] * (a.ndim - 1) + [(lo, width - lo - a.shape[-1])]
    return jnp.pad(a, pad)


def _swap_rope(a):
    half = D_ROPE // 2
    lo = a[..., D_NOPE:D_NOPE + half]
    hi = a[..., D_NOPE + half:D_QK]
    return jnp.concatenate([jnp.zeros_like(a[..., :D_NOPE]), hi, lo,
                            jnp.zeros_like(a[..., D_QK:])], axis=-1)


def _stacked_weights(p):
    depth = p["w_in"].shape[0]
    w_in = p["w_in"]
    o = 0
    parts = {}
    for name, width in (("cq", Q_LORA), ("ckv", KV_LORA), ("kr", D_ROPE), ("u", SSM_WIDTH),
                        ("xq", X_WIDTH), ("g", N_BRANCH * D_MODEL)):
        parts[name] = w_in[:, :, o:o + width]
        o += width
    kr = _pad_lanes(parts["kr"], D_NOPE)
    win = jnp.concatenate([parts["cq"], parts["ckv"], kr, _swap_rope(kr),
                           parts["u"], parts["xq"]], axis=2).astype(BF16)
    wq = _pad_lanes(p["w_q_b"].reshape(depth, Q_LORA, MLA_HEADS, D_QK), 0)
    wq = jnp.concatenate([wq.reshape(depth, Q_LORA, -1),
                          _swap_rope(wq).reshape(depth, Q_LORA, -1)], axis=2)
    wkv = p["w_kv_b"].reshape(depth, KV_LORA, MLA_HEADS, D_NOPE + D_V)
    wk = _pad_lanes(wkv[..., :D_NOPE], 0).reshape(depth, KV_LORA, -1)
    wvt = wkv[..., D_NOPE:].reshape(depth, KV_LORA, MLA_WIDTH).transpose(0, 2, 1)
    row = lambda a: a.reshape(depth, 1, -1)
    gq = _pad_lanes(row(p["q_norm_g"]), 0)
    gk = _pad_lanes(row(p["k_norm_g"]), 0)
    shp = (depth, S5_HALVES, SSM_GROUPS // S5_HALVES, SSM_STATE, SSM_GROUP_CH)
    cshp = (depth, S5_HALVES, SSM_GROUPS // S5_HALVES, SSM_GROUP_CH, SSM_STATE)
    bd = jax.vmap(jax.vmap(_block_diag))
    return {
        "gmix": row(p["norm_mix_g"]),
        "win": win,
        "gqa": row(p["q_a_norm_g"]),
        "wq": wq.astype(BF16),
        "gkva": row(p["kv_a_norm_g"]),
        "wk": wk.astype(BF16),
        "wvt": wvt.astype(BF16),
        "gq": gq,
        "gqs": _swap_rope(gq),
        "gk": gk,
        "gks": _swap_rope(gk),
        "gxq": row(p["xq_norm_g"]),
        "wg": parts["g"].astype(BF16),
        "bg": row(p["b_gate"]),
        "woa": p["w_o_mla"].astype(BF16),
        "wob": p["w_o_ssm"].astype(BF16),
        "woc": p["w_o_cross"].astype(BF16),
        "wout": p["w_out"].astype(BF16),
        "gffn": row(p["norm_ffn_g"]),
        "wu": p["w_up"].astype(BF16),
        "cw": p["conv_w"],
        "cb": row(p["conv_b"]),
        "wd": p["w_down"].astype(BF16),
        "gmem": row(p["mem_norm_g"]),
        "wmem": p["w_mem_kv"].astype(BF16),
        "gxk": row(p["xk_norm_g"]),
        "bmat": jnp.concatenate(
            [bd(p["bb_re"].reshape(shp).transpose(0, 1, 2, 4, 3)),
             bd(p["bb_im"].reshape(shp).transpose(0, 1, 2, 4, 3))], axis=3).astype(BF16),
        "cre": bd(p["ssm_c_re"].reshape(cshp).transpose(0, 1, 2, 4, 3)).astype(BF16),
        "cim": bd(p["ssm_c_im"].reshape(cshp).transpose(0, 1, 2, 4, 3)).astype(BF16),
        "ssm_d": row(p["ssm_d"]),
        "wglu": p["w_glu"].astype(BF16),
        "bglu": row(p["b_glu"]),
    }


def _segment_major(a, bsz, seq):
    c = a.shape[-1]
    a = a.reshape(bsz, seq // S5_CHUNK, SUBLANES, S5_SEG, c)
    return a.transpose(0, 1, 3, 2, 4).reshape(bsz, seq, c)


def _time_major(a, bsz, seq):
    c = a.shape[-1]
    a = a.reshape(bsz, seq // S5_CHUNK, S5_SEG, SUBLANES, c)
    return a.transpose(0, 1, 3, 2, 4).reshape(bsz, seq, c)


def kernel(x, mem, positions, norm_mix_g, w_in, q_a_norm_g, w_q_b, kv_a_norm_g, w_kv_b, q_norm_g, k_norm_g, w_o_mla, ssm_lambda_re, ssm_lambda_im, ssm_log_dt, ssm_b_re, ssm_b_im, ssm_c_re, ssm_c_im, ssm_d, w_glu, b_glu, w_o_ssm, mem_norm_g, w_mem_kv, xq_norm_g, xk_norm_g, w_o_cross, b_gate, w_out, norm_ffn_g, w_up, conv_w, conv_b, w_down):
    bsz, seq, _ = x.shape
    t = bsz * seq
    assert seq % max(TM_FRONT, TM_MIX, TQ, S5_CHUNK) == 0 and t % (ROPE_ROWS * LANES // (D_ROPE // 2)) == 0

    inv_freq = ROPE_THETA ** (-jnp.arange(0, D_ROPE, 2, dtype=F32) / D_ROPE)
    tables = _rope_tables(positions, inv_freq)

    ptab, bb_re, bb_im = _s5_prep(ssm_lambda_re, ssm_lambda_im, ssm_log_dt, ssm_b_re, ssm_b_im)
    w = _stacked_weights(dict(
        norm_mix_g=norm_mix_g, w_in=w_in, q_a_norm_g=q_a_norm_g, w_q_b=w_q_b,
        kv_a_norm_g=kv_a_norm_g, w_kv_b=w_kv_b, q_norm_g=q_norm_g, k_norm_g=k_norm_g,
        w_o_mla=w_o_mla, w_o_ssm=w_o_ssm, w_o_cross=w_o_cross, b_gate=b_gate, w_out=w_out,
        norm_ffn_g=norm_ffn_g, w_up=w_up, conv_w=conv_w, conv_b=conv_b, w_down=w_down,
        xq_norm_g=xq_norm_g, mem_norm_g=mem_norm_g, w_mem_kv=w_mem_kv, xk_norm_g=xk_norm_g,
        bb_re=bb_re, bb_im=bb_im, ssm_c_re=ssm_c_re, ssm_c_im=ssm_c_im, ssm_d=ssm_d,
        w_glu=w_glu, b_glu=b_glu))

    x2 = x.reshape(t, D_MODEL)
    for i in range(DEPTH):
        kx, vx = _memkv(mem, w["gmem"], w["wmem"], w["gxk"], i)
        q, k, vt, u, yc = _front(x2, tables, kx, vx, w, i, bsz, seq)
        ya = _attention(q, k, vt).reshape(t, MLA_WIDTH)
        yb = _s5(_segment_major(u.reshape(bsz, seq, SSM_WIDTH), bsz, seq), w["bmat"], w["cre"],
                 w["cim"], ptab, w["ssm_d"], w["wglu"], w["bglu"], i)
        yb = _time_major(yb, bsz, seq).reshape(t, SSM_WIDTH)
        x2 = _mix(x2, ya, yb, yc, w, i, seq)
    return x2.reshape(bsz, seq, D_MODEL)
```

```python
import functools
import math

import jax
import jax.numpy as jnp
from jax import lax
from jax.experimental import pallas as pl
from jax.experimental.pallas import tpu as pltpu

F32 = jnp.float32
BF16 = jnp.bfloat16

D_MODEL = 1024
DEPTH = 2
MEM_LEN = 256
EPS = 1e-6
MLA_HEADS = 8
Q_LORA = 384
KV_LORA = 256
D_NOPE = 64
D_ROPE = 32
D_QK = D_NOPE + D_ROPE
D_V = 64
MLA_WIDTH = MLA_HEADS * D_V
ROPE_THETA = 10000.0
SSM_GROUPS = 32
SSM_GROUP_CH = 16
SSM_WIDTH = SSM_GROUPS * SSM_GROUP_CH
SSM_STATE = 64
SSM_NS = SSM_GROUPS * SSM_STATE
X_HEADS = 4
X_HEAD_DIM = 128
X_WIDTH = X_HEADS * X_HEAD_DIM
N_BRANCH = 3
D_FF = 2816
CONV_WIDTH = 3

LANES = 128
SUBLANES = 8
HEAD_PAD = LANES

TM_FRONT = 1024
TM_MIX = 512
FFN_CHUNK = 256
DOWN_GROUP = 4
TQ = 512
TK = 256
ATTN_HEADS = 4
V_ROWS = D_V + 16
S5_SEG = 64
S5_CHUNK = SUBLANES * S5_SEG
S5_COLS = 512
S5_HALVES = 2
ROPE_ROWS = 512

VMEM_LIMIT = 56 * 1024 * 1024


def _layer_spec(arr, layer):
    nd = arr.ndim - 1
    return pl.BlockSpec((None,) + arr.shape[1:], lambda *_: (layer,) + (0,) * nd,
                        pipeline_mode=pl.Buffered(1))


def _rms(x, g):
    return x * lax.rsqrt(jnp.mean(x * x, axis=-1, keepdims=True) + EPS) * g


def _dot(a, b):
    return jnp.dot(a, b, preferred_element_type=F32)


def _dot_nt(a, b):
    return lax.dot_general(a, b, (((1,), (1,)), ((), ())), preferred_element_type=F32)


def _params(sem, limit=VMEM_LIMIT):
    return pltpu.CompilerParams(dimension_semantics=sem, vmem_limit_bytes=limit)


def _rope_kernel(pos_ref, invf_ref, cos_ref, sin_ref, nsin_ref):
    ang = pos_ref[...].astype(F32) * invf_ref[...]
    s = jnp.sin(ang)
    cos_ref[...] = jnp.cos(ang)
    sin_ref[...] = s
    nsin_ref[...] = -s


def _rope_tables(positions, inv_freq):
    t = positions.size
    half = D_ROPE // 2
    per_row = LANES // half
    rows = t // per_row
    pos = jnp.repeat(positions.reshape(rows, per_row), half, axis=1)
    invf = jnp.tile(inv_freq, per_row).reshape(1, 1, LANES)
    out = jax.ShapeDtypeStruct((rows, LANES), F32)
    spec = pl.BlockSpec((ROPE_ROWS, LANES), lambda i: (i, 0))
    cos, sin, nsin = pl.pallas_call(
        _rope_kernel,
        grid=(rows // ROPE_ROWS,),
        in_specs=[spec, _layer_spec(invf, 0)],
        out_specs=[spec, spec, spec],
        out_shape=[out, out, out],
        compiler_params=_params(("arbitrary",)),
        name="rope_tables",
    )(pos, invf)
    cos, sin, nsin = (a.reshape(t, half) for a in (cos, sin, nsin))
    cc = jnp.concatenate([jnp.ones((t, D_NOPE), F32), cos, cos,
                          jnp.zeros((t, HEAD_PAD - D_QK), F32)], axis=1)
    ss = jnp.concatenate([jnp.zeros((t, D_NOPE), F32), nsin, sin,
                          jnp.zeros((t, HEAD_PAD - D_QK), F32)], axis=1)
    return cc, ss


def _memkv_kernel(mem_ref, g_ref, w_ref, gk_ref, k_ref, v_ref):
    m = _rms(mem_ref[0], g_ref[...]).astype(BF16)
    kv = _dot(m, w_ref[...])
    for h in range(X_HEADS):
        sl = slice(h * X_HEAD_DIM, (h + 1) * X_HEAD_DIM)
        k_ref[0, :, sl] = _rms(kv[:, sl], gk_ref[...]).astype(BF16)
    v_ref[0] = kv[:, X_WIDTH:].astype(BF16)


def _memkv(mem, g, w, gk, layer):
    b = mem.shape[0]
    out = jax.ShapeDtypeStruct((b, MEM_LEN, X_WIDTH), BF16)
    spec = pl.BlockSpec((1, MEM_LEN, X_WIDTH), lambda i: (i, 0, 0))
    return pl.pallas_call(
        _memkv_kernel,
        grid=(b,),
        in_specs=[pl.BlockSpec((1, MEM_LEN, D_MODEL), lambda i: (i, 0, 0)),
                  _layer_spec(g, layer), _layer_spec(w, layer), _layer_spec(gk, layer)],
        out_specs=[spec, spec],
        out_shape=[out, out],
        compiler_params=_params(("arbitrary",)),
        name="mem_kv",
    )(mem, g, w, gk)


def _front_kernel(x_ref, cc_ref, ss_ref, kx_ref, vx_ref, gmix_ref, win_ref,
                  gqa_ref, wq_ref, gkva_ref, wk_ref, wvt_ref, gq_ref, gqs_ref, gk_ref, gks_ref,
                  gxq_ref, q_ref, k_ref, vt_ref, u_ref, yc_ref):
    tm = x_ref.shape[0]
    h = _rms(x_ref[...], gmix_ref[...]).astype(BF16)
    p = _dot(h, win_ref[...])
    o_ckv = Q_LORA
    o_kr = o_ckv + KV_LORA
    o_krs = o_kr + HEAD_PAD
    o_u = o_krs + HEAD_PAD
    o_xq = o_u + SSM_WIDTH
    cq = p[:, :o_ckv]
    ckv = p[:, o_ckv:o_kr]
    kr = p[:, o_kr:o_krs]
    krs = p[:, o_krs:o_u]
    u_ref[...] = p[:, o_u:o_xq].astype(BF16)
    xq = p[:, o_xq:]

    cqn = _rms(cq, gqa_ref[...]).astype(BF16)
    ckvn = _rms(ckv, gkva_ref[...]).astype(BF16)

    vt = _dot_nt(wvt_ref[...], ckvn)
    ones = jnp.ones((V_ROWS - D_V, TK), BF16)
    for hd in range(MLA_HEADS):
        for c in range(tm // TK):
            vt_ref[0, hd, c, :D_V, :] = vt[hd * D_V:(hd + 1) * D_V, c * TK:(c + 1) * TK].astype(BF16)
            vt_ref[0, hd, c, D_V:, :] = ones

    cc = cc_ref[...]
    ss = ss_ref[...]
    nq = MLA_HEADS * HEAD_PAD

    def inv_rms(a):
        return lax.rsqrt(jnp.sum(a * a, axis=-1, keepdims=True) * (1.0 / D_QK) + EPS)

    qa = _dot(cqn, wq_ref[...])
    gc = gq_ref[...] * cc
    gs = gqs_ref[...] * ss
    scale = D_QK ** -0.5 * math.log2(math.e)
    for hd in range(MLA_HEADS):
        a = qa[:, hd * HEAD_PAD:(hd + 1) * HEAD_PAD]
        a_s = qa[:, nq + hd * HEAD_PAD:nq + (hd + 1) * HEAD_PAD]
        q_ref[0, hd] = ((inv_rms(a) * scale) * (a * gc + a_s * gs)).astype(BF16)

    ka = _dot(ckvn, wk_ref[...])
    gc = gk_ref[...] * cc
    rot = krs * (gks_ref[...] * ss)
    for hd in range(MLA_HEADS):
        a = ka[:, hd * HEAD_PAD:(hd + 1) * HEAD_PAD] + kr
        k_ref[0, hd] = (inv_rms(a) * (a * gc + rot)).astype(BF16)

    xscale = X_HEAD_DIM ** -0.5
    for hd in range(X_HEADS):
        sl = slice(hd * X_HEAD_DIM, (hd + 1) * X_HEAD_DIM)
        qx = (_rms(xq[:, sl], gxq_ref[...]) * xscale).astype(BF16)
        s = _dot_nt(qx, kx_ref[0, :, sl])
        pm = jnp.exp(s - jnp.max(s, axis=-1, keepdims=True))
        l = jnp.sum(pm, axis=-1, keepdims=True)
        yc_ref[:, sl] = (_dot(pm.astype(BF16), vx_ref[0, :, sl]) / l).astype(BF16)


def _front(x2, tables, kx, vx, w, layer, bsz, seq):
    t = bsz * seq
    tm = TM_FRONT
    nb = seq // tm
    row = lambda c: pl.BlockSpec((tm, c), lambda i: (i, 0))
    hspec = pl.BlockSpec((1, MLA_HEADS, tm, HEAD_PAD), lambda i: (i // nb, 0, i % nb, 0))
    mspec = pl.BlockSpec((1, MEM_LEN, X_WIDTH), lambda i: (i // nb, 0, 0))
    consts = [w["gmix"], w["win"], w["gqa"], w["wq"], w["gkva"], w["wk"], w["wvt"],
              w["gq"], w["gqs"], w["gk"], w["gks"], w["gxq"]]
    hshape = jax.ShapeDtypeStruct((bsz, MLA_HEADS, seq, HEAD_PAD), BF16)
    tshape = jax.ShapeDtypeStruct((t, SSM_WIDTH), BF16)
    vtshape = jax.ShapeDtypeStruct((bsz, MLA_HEADS, seq // TK, V_ROWS, TK), BF16)
    vtspec = pl.BlockSpec((1, MLA_HEADS, tm // TK, V_ROWS, TK), lambda i: (i // nb, 0, i % nb, 0, 0))
    return pl.pallas_call(
        _front_kernel,
        grid=(t // tm,),
        in_specs=[row(D_MODEL), row(HEAD_PAD), row(HEAD_PAD), mspec, mspec]
                 + [_layer_spec(c, layer) for c in consts],
        out_specs=[hspec, hspec, vtspec, row(SSM_WIDTH), row(X_WIDTH)],
        out_shape=[hshape, hshape, vtshape, tshape, tshape],
        compiler_params=_params(("arbitrary",)),
        name="front",
    )(x2, *tables, kx, vx, *consts)


def _attn_kernel(q_ref, k_ref, vt_ref, o_ref, st_scr, m_scr, acc_scr):
    pair = pl.program_id(2)
    per = TQ // TK

    def scores(kb, slot, hh, qoff):
        r = pl.multiple_of(kb * TQ, TQ)
        st_scr[slot, hh] = _dot_nt(k_ref[0, hh, pl.ds(r, TQ), :], q_ref[0, hh, qoff:qoff + TQ, :])

    def consume(kb, slot, hh, masked):
        st = st_scr[slot, hh]
        if masked:
            key = lax.broadcasted_iota(jnp.int32, st.shape, 0)
            qry = lax.broadcasted_iota(jnp.int32, st.shape, 1)
            st = jnp.where(key <= qry, st, -jnp.inf)
        m_old = m_scr[hh]
        m_new = jnp.maximum(m_old, jnp.max(st, axis=0, keepdims=True))
        pm = jnp.exp2(st - m_new).astype(BF16)
        alpha = jnp.exp2(m_old - m_new)
        vt = jnp.concatenate([vt_ref[0, hh, kb * per + i] for i in range(per)], axis=1)
        acc_scr[hh] = alpha * acc_scr[hh] + _dot(vt, pm)
        m_scr[hh] = m_new

    def step(kb, slot, qoff, masked=False, ahead=None):
        for hh in range(ATTN_HEADS):
            if ahead is not None:
                scores(ahead[0], 1 - slot, hh, ahead[1])
            consume(kb, slot, hh, masked)

    def reset():
        m_scr[...] = jnp.full(m_scr.shape, -jnp.inf, F32)
        acc_scr[...] = jnp.zeros(acc_scr.shape, F32)

    def finish(qoff):
        outs = []
        for hh in range(ATTN_HEADS):
            a = acc_scr[hh]
            outs.append(a[:D_V] / a[D_V:D_V + 1])
        o_ref[0, qoff:qoff + TQ, :] = jnp.concatenate(outs, axis=0).T.astype(BF16)

    reset()
    for hh in range(ATTN_HEADS):
        scores(0, 0, hh, 0)

    def even_body(j, c):
        step(2 * j, 0, 0, ahead=(2 * j + 1, 0))
        step(2 * j + 1, 1, 0, ahead=(2 * j + 2, 0))
        return c

    lax.fori_loop(0, pair, even_body, 0)
    step(2 * pair, 0, 0, masked=True, ahead=(0, TQ))
    finish(0)

    reset()

    def odd_body(j, c):
        step(2 * j, 1, TQ, ahead=(2 * j + 1, TQ))
        step(2 * j + 1, 0, TQ, ahead=(2 * j + 2, TQ))
        return c

    lax.fori_loop(0, pair, odd_body, 0)
    step(2 * pair, 1, TQ, ahead=(2 * pair + 1, TQ))
    step(2 * pair + 1, 0, TQ, masked=True)
    finish(TQ)


def _attention(q, k, vt):
    bsz, nh, seq, _ = q.shape
    assert TQ % TK == 0
    nh_blk = ATTN_HEADS
    tq2 = 2 * TQ
    return pl.pallas_call(
        _attn_kernel,
        grid=(bsz, nh // nh_blk, seq // tq2),
        in_specs=[pl.BlockSpec((1, nh_blk, tq2, HEAD_PAD), lambda b, hg, i: (b, hg, i, 0)),
                  pl.BlockSpec((1, nh_blk, seq, HEAD_PAD), lambda b, hg, i: (b, hg, 0, 0)),
                  pl.BlockSpec((1, nh_blk, seq // TK, V_ROWS, TK), lambda b, hg, i: (b, hg, 0, 0, 0))],
        out_specs=pl.BlockSpec((1, tq2, nh_blk * D_V), lambda b, hg, i: (b, i, hg)),
        out_shape=jax.ShapeDtypeStruct((bsz, seq, nh * D_V), BF16),
        scratch_shapes=[pltpu.VMEM((2, nh_blk, TQ, TQ), F32), pltpu.VMEM((nh_blk, 1, TQ), F32),
                        pltpu.VMEM((nh_blk, V_ROWS, TQ), F32)],
        compiler_params=_params(("arbitrary", "arbitrary", "arbitrary")),
        name="mla_attention",
    )(q, k, vt)


def _s5_prep_kernel(lr_ref, li_ref, ldt_ref, lrc_ref, lic_ref, ldtc_ref, bre_ref, bim_ref,
                    ptab_ref, bbre_ref, bbim_ref):
    ns = SSM_NS
    dt = jnp.exp(ldt_ref[0])
    lr = lr_ref[0] * dt
    li = li_ref[0] * dt
    k = (lax.broadcasted_iota(jnp.int32, (S5_SEG, 1), 0) + 1).astype(F32)
    mag = jnp.exp(k * lr)
    ang = k * li
    p_re = mag * jnp.cos(ang)
    p_im = mag * jnp.sin(ang)
    for r in range(S5_SEG):
        rows = slice(r * SUBLANES, (r + 1) * SUBLANES)
        ptab_ref[0, rows, :ns] = jnp.broadcast_to(p_re[r:r + 1], (SUBLANES, ns))
        ptab_ref[0, rows, ns:] = jnp.broadcast_to(p_im[r:r + 1], (SUBLANES, ns))
    dtc = jnp.exp(ldtc_ref[0])
    lrc = lrc_ref[0]
    lic = lic_ref[0]
    magc = jnp.exp(lrc * dtc)
    e_re = magc * jnp.cos(lic * dtc) - 1.0
    e_im = magc * jnp.sin(lic * dtc)
    den = lrc * lrc + lic * lic
    f_re = (e_re * lrc + e_im * lic) / den
    f_im = (e_im * lrc - e_re * lic) / den
    bre = bre_ref[0]
    bim = bim_ref[0]
    bbre_ref[0] = f_re * bre - f_im * bim
    bbim_ref[0] = f_re * bim + f_im * bre


def _s5_prep(lam_re, lam_im, log_dt, b_re, b_im):
    depth = lam_re.shape[0]
    ns = SSM_NS
    ldt = jnp.broadcast_to(log_dt[:, :, None], lam_re.shape)
    rowv = lambda a: a.reshape(depth, 1, ns)
    colv = lambda a: a.reshape(depth, ns, 1)
    bcol = lambda a: a.reshape(depth, ns, SSM_GROUP_CH)
    spec = lambda s: pl.BlockSpec((1,) + s, lambda i: (i, 0, 0))
    return pl.pallas_call(
        _s5_prep_kernel,
        grid=(depth,),
        in_specs=[spec((1, ns))] * 3 + [spec((ns, 1))] * 3 + [spec((ns, SSM_GROUP_CH))] * 2,
        out_specs=[spec((S5_CHUNK, 2 * ns)),
                   spec((ns, SSM_GROUP_CH)), spec((ns, SSM_GROUP_CH))],
        out_shape=[jax.ShapeDtypeStruct((depth, S5_CHUNK, 2 * ns), F32),
                   jax.ShapeDtypeStruct((depth, ns, SSM_GROUP_CH), F32),
                   jax.ShapeDtypeStruct((depth, ns, SSM_GROUP_CH), F32)],
        compiler_params=_params(("arbitrary",)),
        name="s5_prep",
    )(rowv(lam_re), rowv(lam_im), rowv(ldt), colv(lam_re), colv(lam_im), colv(ldt),
      bcol(b_re), bcol(b_im))


def _s5_kernel(u_ref, bmat_ref, cre_ref, cim_ref, ptab_ref, d_ref, wglu_ref, bglu_ref,
               o_ref, s_scr, carry_scr, c_scr):
    ns = SSM_NS
    cw = S5_COLS
    nb = u_ref.shape[0]
    hw = SSM_WIDTH // S5_HALVES
    hs = ns // S5_HALVES

    @pl.when(pl.program_id(0) == 0)
    def _():
        carry_scr[...] = jnp.zeros(carry_scr.shape, F32)

    for b in range(nb):
        for hf in range(S5_HALVES):
            bu = _dot(u_ref[b, :, hf * hw:(hf + 1) * hw], bmat_ref[hf])
            s_scr[b, :, hf * hs:(hf + 1) * hs] = bu[:, :hs]
            s_scr[b, :, ns + hf * hs:ns + (hf + 1) * hs] = bu[:, hs:]

    cols = lambda cg: (slice(cg * cw, (cg + 1) * cw), slice(ns + cg * cw, ns + (cg + 1) * cw))
    ncg = ns // cw

    for b in range(nb):
        z = {}
        for i in range(S5_SEG):
            rows = slice(i * SUBLANES, (i + 1) * SUBLANES)
            for cg in range(ncg):
                re, im = cols(cg)
                if i == 0:
                    z[cg] = (s_scr[b, rows, re], s_scr[b, rows, im])
                    continue
                a_re = ptab_ref[:SUBLANES, re]
                a_im = ptab_ref[:SUBLANES, im]
                z_re, z_im = z[cg]
                n_re = a_re * z_re - a_im * z_im + s_scr[b, rows, re]
                n_im = a_re * z_im + a_im * z_re + s_scr[b, rows, im]
                s_scr[b, rows, re] = n_re
                s_scr[b, rows, im] = n_im
                z[cg] = (n_re, n_im)

        c = {}
        for cg in range(ncg):
            re, im = cols(cg)
            e_re, e_im = z[cg]
            p_re = ptab_ref[S5_CHUNK - 1:S5_CHUNK, re]
            p_im = ptab_ref[S5_CHUNK - 1:S5_CHUNK, im]
            cur_re = carry_scr[b, :, re]
            cur_im = carry_scr[b, :, im]
            for j in range(SUBLANES):
                c_scr[b, j:j + 1, re] = cur_re
                c_scr[b, j:j + 1, im] = cur_im
                n_re = e_re[j:j + 1, :] + p_re * cur_re - p_im * cur_im
                n_im = e_im[j:j + 1, :] + p_re * cur_im + p_im * cur_re
                cur_re, cur_im = n_re, n_im
            carry_scr[b, :, re] = cur_re
            carry_scr[b, :, im] = cur_im
            c[cg] = (c_scr[b, :, re], c_scr[b, :, im])

        for i in range(S5_SEG):
            rows = slice(i * SUBLANES, (i + 1) * SUBLANES)
            for cg in range(ncg):
                re, im = cols(cg)
                c_re, c_im = c[cg]
                q_re = ptab_ref[rows, re]
                q_im = ptab_ref[rows, im]
                s_scr[b, rows, re] = s_scr[b, rows, re] + q_re * c_re - q_im * c_im
                s_scr[b, rows, im] = s_scr[b, rows, im] + q_re * c_im + q_im * c_re

    for b in range(nb):
        ys = []
        for hf in range(S5_HALVES):
            s_re = s_scr[b, :, hf * hs:(hf + 1) * hs].astype(BF16)
            s_im = s_scr[b, :, ns + hf * hs:ns + (hf + 1) * hs].astype(BF16)
            ys.append(_dot(s_re, cre_ref[hf]) - _dot(s_im, cim_ref[hf]))
        y = jnp.concatenate(ys, axis=1) + d_ref[...] * u_ref[b].astype(F32)
        y = jax.nn.gelu(y)
        zz = _dot(y.astype(BF16), wglu_ref[...]) + bglu_ref[...]
        o_ref[b] = (y * jax.nn.sigmoid(zz)).astype(BF16)


def _s5(u_perm, bmat, cre, cim, ptab, d, wglu, bglu, layer):
    bsz, seq, _ = u_perm.shape
    ns = SSM_NS
    blk = pl.BlockSpec((bsz, S5_CHUNK, SSM_WIDTH), lambda j: (0, j, 0))
    consts = [bmat, cre, cim, ptab, d, wglu, bglu]
    return pl.pallas_call(
        _s5_kernel,
        grid=(seq // S5_CHUNK,),
        in_specs=[blk] + [_layer_spec(c, layer) for c in consts],
        out_specs=blk,
        out_shape=jax.ShapeDtypeStruct((bsz, seq, SSM_WIDTH), BF16),
        scratch_shapes=[pltpu.VMEM((bsz, S5_CHUNK, 2 * ns), F32), pltpu.VMEM((bsz, 1, 2 * ns), F32),
                        pltpu.VMEM((bsz, SUBLANES, 2 * ns), F32)],
        compiler_params=_params(("arbitrary",)),
        name="s5_scan",
    )(u_perm, *consts)


def _mix_kernel(x_ref, ya_ref, yb_ref, yc_ref, gmix_ref, wg_ref, bg_ref, woa_ref, wob_ref,
                woc_ref, wout_ref, g_ref, wu_ref, cw_ref, cb_ref, wd_ref, o_ref, carry_g, carry_v,
                *, blocks_per_seq):
    tm = x_ref.shape[0]

    @pl.when(pl.program_id(0) % blocks_per_seq == 0)
    def _():
        carry_g[...] = jnp.zeros(carry_g.shape, F32)
        carry_v[...] = jnp.zeros(carry_v.shape, F32)

    x = x_ref[...]
    h = _rms(x, gmix_ref[...]).astype(BF16)
    merged = None
    for br, (y_ref, wo_ref) in enumerate(((ya_ref, woa_ref), (yb_ref, wob_ref), (yc_ref, woc_ref))):
        sl = slice(br * D_MODEL, (br + 1) * D_MODEL)
        gate = jax.nn.sigmoid(_dot(h, wg_ref[:, sl]) + bg_ref[:, sl])
        term = gate * _dot(y_ref[...], wo_ref[...])
        merged = term if merged is None else merged + term
    x = x + _dot(merged.astype(BF16), wout_ref[...])

    h2 = _rms(x, g_ref[...]).astype(BF16)
    row = lax.broadcasted_iota(jnp.int32, (tm, 1), 0)

    def up_proj(c):
        lo = c * FFN_CHUNK
        return (_dot(h2, wu_ref[:, lo:lo + FFN_CHUNK]),
                _dot(h2, wu_ref[:, D_FF + lo:D_FF + lo + FFN_CHUNK]))

    def conv(cs, up, off, carry):
        ws = slice(off + cs.start, off + cs.stop)
        prev = carry[:, cs]
        carry[:, cs] = up[tm - SUBLANES:, :]
        p1 = prev[SUBLANES - 1:SUBLANES, :]
        p2 = prev[SUBLANES - 2:SUBLANES - 1, :]
        m1 = jnp.where(row == 0, p1, pltpu.roll(up, 1, 0))
        m2 = jnp.where(row == 0, p2, jnp.where(row == 1, p1, pltpu.roll(up, 2, 0)))
        return cw_ref[0:1, ws] * m2 + cw_ref[1:2, ws] * m1 + cw_ref[2:3, ws] * up + cb_ref[:, ws]

    nchunk = D_FF // FFN_CHUNK
    acc = jnp.zeros((tm, D_MODEL), F32)
    ahead = 2
    ups = [up_proj(c) for c in range(ahead)]
    acts = []
    for c in range(nchunk):
        if c + ahead < nchunk:
            ups.append(up_proj(c + ahead))
        cs = slice(c * FFN_CHUNK, (c + 1) * FFN_CHUNK)
        gv = conv(cs, ups[c][0], 0, carry_g)
        vv = conv(cs, ups[c][1], D_FF, carry_v)
        acts.append((gv * jax.nn.sigmoid(gv) * vv).astype(BF16))
        if len(acts) == DOWN_GROUP or c + 1 == nchunk:
            lo = (c + 1 - len(acts)) * FFN_CHUNK
            act = acts[0] if len(acts) == 1 else jnp.concatenate(acts, axis=1)
            acc = acc + _dot(act, wd_ref[lo:(c + 1) * FFN_CHUNK, :])
            acts = []
    o_ref[...] = x + acc


def _mix(x2, ya, yb, yc, w, layer, seq):
    t = x2.shape[0]
    tm = TM_MIX
    row = lambda c: pl.BlockSpec((tm, c), lambda i: (i, 0))
    consts = [w["gmix"], w["wg"], w["bg"], w["woa"], w["wob"], w["woc"], w["wout"],
              w["gffn"], w["wu"], w["cw"], w["cb"], w["wd"]]
    return pl.pallas_call(
        functools.partial(_mix_kernel, blocks_per_seq=seq // tm),
        grid=(t // tm,),
        in_specs=[row(D_MODEL), row(MLA_WIDTH), row(SSM_WIDTH), row(X_WIDTH)]
                 + [_layer_spec(c, layer) for c in consts],
        out_specs=row(D_MODEL),
        out_shape=jax.ShapeDtypeStruct((t, D_MODEL), F32),
        scratch_shapes=[pltpu.VMEM((SUBLANES, D_FF), F32), pltpu.VMEM((SUBLANES, D_FF), F32)],
        compiler_params=_params(("arbitrary",)),
        name="mix",
    )(x2, ya, yb, yc, *consts)


def _block_diag(blocks):
    g, r, c = blocks.shape
    eye = jnp.eye(g, dtype=bool)
    full = jnp.where(eye[:, None, :, None], blocks[:, :, None, :], jnp.zeros((), blocks.dtype))
    return full.reshape(g * r, g * c)


def _pad_lanes(a, lo, width=HEAD_PAD):
    pad = [(0, 0)] * (a.ndim - 1) + [(lo, width - lo - a.shape[-1])]
    return jnp.pad(a, pad)


def _swap_rope(a):
    half = D_ROPE // 2
    lo = a[..., D_NOPE:D_NOPE + half]
    hi = a[..., D_NOPE + half:D_QK]
    return jnp.concatenate([jnp.zeros_like(a[..., :D_NOPE]), hi, lo,
                            jnp.zeros_like(a[..., D_QK:])], axis=-1)


def _stacked_weights(p):
    depth = p["w_in"].shape[0]
    w_in = p["w_in"]
    o = 0
    parts = {}
    for name, width in (("cq", Q_LORA), ("ckv", KV_LORA), ("kr", D_ROPE), ("u", SSM_WIDTH),
                        ("xq", X_WIDTH), ("g", N_BRANCH * D_MODEL)):
        parts[name] = w_in[:, :, o:o + width]
        o += width
    kr = _pad_lanes(parts["kr"], D_NOPE)
    win = jnp.concatenate([parts["cq"], parts["ckv"], kr, _swap_rope(kr),
                           parts["u"], parts["xq"]], axis=2).astype(BF16)
    wq = _pad_lanes(p["w_q_b"].reshape(depth, Q_LORA, MLA_HEADS, D_QK), 0)
    wq = jnp.concatenate([wq.reshape(depth, Q_LORA, -1),
                          _swap_rope(wq).reshape(depth, Q_LORA, -1)], axis=2)
    wkv = p["w_kv_b"].reshape(depth, KV_LORA, MLA_HEADS, D_NOPE + D_V)
    wk = _pad_lanes(wkv[..., :D_NOPE], 0).reshape(depth, KV_LORA, -1)
    wvt = wkv[..., D_NOPE:].reshape(depth, KV_LORA, MLA_WIDTH).transpose(0, 2, 1)
    row = lambda a: a.reshape(depth, 1, -1)
    gq = _pad_lanes(row(p["q_norm_g"]), 0)
    gk = _pad_lanes(row(p["k_norm_g"]), 0)
    shp = (depth, S5_HALVES, SSM_GROUPS // S5_HALVES, SSM_STATE, SSM_GROUP_CH)
    cshp = (depth, S5_HALVES, SSM_GROUPS // S5_HALVES, SSM_GROUP_CH, SSM_STATE)
    bd = jax.vmap(jax.vmap(_block_diag))
    return {
        "gmix": row(p["norm_mix_g"]),
        "win": win,
        "gqa": row(p["q_a_norm_g"]),
        "wq": wq.astype(BF16),
        "gkva": row(p["kv_a_norm_g"]),
        "wk": wk.astype(BF16),
        "wvt": wvt.astype(BF16),
        "gq": gq,
        "gqs": _swap_rope(gq),
        "gk": gk,
        "gks": _swap_rope(gk),
        "gxq": row(p["xq_norm_g"]),
        "wg": parts["g"].astype(BF16),
        "bg": row(p["b_gate"]),
        "woa": p["w_o_mla"].astype(BF16),
        "wob": p["w_o_ssm"].astype(BF16),
        "woc": p["w_o_cross"].astype(BF16),
        "wout": p["w_out"].astype(BF16),
        "gffn": row(p["norm_ffn_g"]),
        "wu": p["w_up"].astype(BF16),
        "cw": p["conv_w"],
        "cb": row(p["conv_b"]),
        "wd": p["w_down"].astype(BF16),
        "gmem": row(p["mem_norm_g"]),
        "wmem": p["w_mem_kv"].astype(BF16),
        "gxk": row(p["xk_norm_g"]),
        "bmat": jnp.concatenate(
            [bd(p["bb_re"].reshape(shp).transpose(0, 1, 2, 4, 3)),
             bd(p["bb_im"].reshape(shp).transpose(0, 1, 2, 4, 3))], axis=3).astype(BF16),
        "cre": bd(p["ssm_c_re"].reshape(cshp).transpose(0, 1, 2, 4, 3)).astype(BF16),
        "cim": bd(p["ssm_c_im"].reshape(cshp).transpose(0, 1, 2, 4, 3)).astype(BF16),
        "ssm_d": row(p["ssm_d"]),
        "wglu": p["w_glu"].astype(BF16),
        "bglu": row(p["b_glu"]),
    }


def _segment_major(a, bsz, seq):
    c = a.shape[-1]
    a = a.reshape(bsz, seq // S5_CHUNK, SUBLANES, S5_SEG, c)
    return a.transpose(0, 1, 3, 2, 4).reshape(bsz, seq, c)


def _time_major(a, bsz, seq):
    c = a.shape[-1]
    a = a.reshape(bsz, seq // S5_CHUNK, S5_SEG, SUBLANES, c)
    return a.transpose(0, 1, 3, 2, 4).reshape(bsz, seq, c)


def kernel(x, mem, positions, norm_mix_g, w_in, q_a_norm_g, w_q_b, kv_a_norm_g, w_kv_b, q_norm_g, k_norm_g, w_o_mla, ssm_lambda_re, ssm_lambda_im, ssm_log_dt, ssm_b_re, ssm_b_im, ssm_c_re, ssm_c_im, ssm_d, w_glu, b_glu, w_o_ssm, mem_norm_g, w_mem_kv, xq_norm_g, xk_norm_g, w_o_cross, b_gate, w_out, norm_ffn_g, w_up, conv_w, conv_b, w_down):
    bsz, seq, _ = x.shape
    t = bsz * seq
    assert seq % max(TM_FRONT, TM_MIX, 2 * TQ, S5_CHUNK) == 0 and t % (ROPE_ROWS * LANES // (D_ROPE // 2)) == 0

    inv_freq = ROPE_THETA ** (-jnp.arange(0, D_ROPE, 2, dtype=F32) / D_ROPE)
    tables = _rope_tables(positions, inv_freq)

    ptab, bb_re, bb_im = _s5_prep(ssm_lambda_re, ssm_lambda_im, ssm_log_dt, ssm_b_re, ssm_b_im)
    w = _stacked_weights(dict(
        norm_mix_g=norm_mix_g, w_in=w_in, q_a_norm_g=q_a_norm_g, w_q_b=w_q_b,
        kv_a_norm_g=kv_a_norm_g, w_kv_b=w_kv_b, q_norm_g=q_norm_g, k_norm_g=k_norm_g,
        w_o_mla=w_o_mla, w_o_ssm=w_o_ssm, w_o_cross=w_o_cross, b_gate=b_gate, w_out=w_out,
        norm_ffn_g=norm_ffn_g, w_up=w_up, conv_w=conv_w, conv_b=conv_b, w_down=w_down,
        xq_norm_g=xq_norm_g, mem_norm_g=mem_norm_g, w_mem_kv=w_mem_kv, xk_norm_g=xk_norm_g,
        bb_re=bb_re, bb_im=bb_im, ssm_c_re=ssm_c_re, ssm_c_im=ssm_c_im, ssm_d=ssm_d,
        w_glu=w_glu, b_glu=b_glu))

    x2 = x.reshape(t, D_MODEL)
    for i in range(DEPTH):
        kx, vx = _memkv(mem, w["gmem"], w["wmem"], w["gxk"], i)
        q, k, vt, u, yc = _front(x2, tables, kx, vx, w, i, bsz, seq)
        ya = _attention(q, k, vt).reshape(t, MLA_WIDTH)
        yb = _s5(_segment_major(u.reshape(bsz, seq, SSM_WIDTH), bsz, seq), w["bmat"], w["cre"],
                 w["cim"], ptab, w["ssm_d"], w["wglu"], w["bglu"], i)
        yb = _time_major(yb, bsz, seq).reshape(t, SSM_WIDTH)
        x2 = _mix(x2, ya, yb, yc, w, i, seq)
    return x2.reshape(bsz, seq, D_MODEL)
```

```python
import functools
import math

import jax
import jax.numpy as jnp
from jax import lax
from jax.experimental import pallas as pl
from jax.experimental.pallas import tpu as pltpu

F32 = jnp.float32
BF16 = jnp.bfloat16

D_MODEL = 1024
DEPTH = 2
MEM_LEN = 256
EPS = 1e-6
MLA_HEADS = 8
Q_LORA = 384
KV_LORA = 256
D_NOPE = 64
D_ROPE = 32
D_QK = D_NOPE + D_ROPE
D_V = 64
MLA_WIDTH = MLA_HEADS * D_V
ROPE_THETA = 10000.0
SSM_GROUPS = 32
SSM_GROUP_CH = 16
SSM_WIDTH = SSM_GROUPS * SSM_GROUP_CH
SSM_STATE = 64
SSM_NS = SSM_GROUPS * SSM_STATE
X_HEADS = 4
X_HEAD_DIM = 128
X_WIDTH = X_HEADS * X_HEAD_DIM
N_BRANCH = 3
D_FF = 2816
CONV_WIDTH = 3

LANES = 128
SUBLANES = 8
HEAD_PAD = LANES

TM_FRONT = 1024
TM_MIX = 512
FFN_CHUNK = 256
DOWN_GROUP = 4
TQ = 512
TK = 256
ATTN_HEADS = 4
V_ROWS = D_V + 16
S5_SEG = 64
S5_CHUNK = SUBLANES * S5_SEG
S5_COLS = 512
S5_HALVES = 2
ROPE_ROWS = 512

VMEM_LIMIT = 56 * 1024 * 1024


def _layer_spec(arr, layer):
    nd = arr.ndim - 1
    return pl.BlockSpec((None,) + arr.shape[1:], lambda *_: (layer,) + (0,) * nd,
                        pipeline_mode=pl.Buffered(1))


def _rms(x, g):
    return x * lax.rsqrt(jnp.mean(x * x, axis=-1, keepdims=True) + EPS) * g


def _dot(a, b):
    return jnp.dot(a, b, preferred_element_type=F32)


def _dot_nt(a, b):
    return lax.dot_general(a, b, (((1,), (1,)), ((), ())), preferred_element_type=F32)


def _params(sem, limit=VMEM_LIMIT):
    return pltpu.CompilerParams(dimension_semantics=sem, vmem_limit_bytes=limit)


def _rope_kernel(pos_ref, invf_ref, cos_ref, sin_ref, nsin_ref):
    ang = pos_ref[...].astype(F32) * invf_ref[...]
    s = jnp.sin(ang)
    cos_ref[...] = jnp.cos(ang)
    sin_ref[...] = s
    nsin_ref[...] = -s


def _rope_tables(positions, inv_freq):
    t = positions.size
    half = D_ROPE // 2
    per_row = LANES // half
    rows = t // per_row
    pos = jnp.repeat(positions.reshape(rows, per_row), half, axis=1)
    invf = jnp.tile(inv_freq, per_row).reshape(1, 1, LANES)
    out = jax.ShapeDtypeStruct((rows, LANES), F32)
    spec = pl.BlockSpec((ROPE_ROWS, LANES), lambda i: (i, 0))
    cos, sin, nsin = pl.pallas_call(
        _rope_kernel,
        grid=(rows // ROPE_ROWS,),
        in_specs=[spec, _layer_spec(invf, 0)],
        out_specs=[spec, spec, spec],
        out_shape=[out, out, out],
        compiler_params=_params(("arbitrary",)),
        name="rope_tables",
    )(pos, invf)
    cos, sin, nsin = (a.reshape(t, half) for a in (cos, sin, nsin))
    cc = jnp.concatenate([jnp.ones((t, D_NOPE), F32), cos, cos,
                          jnp.zeros((t, HEAD_PAD - D_QK), F32)], axis=1)
    ss = jnp.concatenate([jnp.zeros((t, D_NOPE), F32), nsin, sin,
                          jnp.zeros((t, HEAD_PAD - D_QK), F32)], axis=1)
    return cc, ss


def _memkv_kernel(mem_ref, g_ref, w_ref, gk_ref, k_ref, v_ref):
    m = _rms(mem_ref[0], g_ref[...]).astype(BF16)
    kv = _dot(m, w_ref[...])
    for h in range(X_HEADS):
        sl = slice(h * X_HEAD_DIM, (h + 1) * X_HEAD_DIM)
        k_ref[0, :, sl] = _rms(kv[:, sl], gk_ref[...]).astype(BF16)
    v_ref[0] = kv[:, X_WIDTH:].astype(BF16)


def _memkv(mem, g, w, gk, layer):
    b = mem.shape[0]
    out = jax.ShapeDtypeStruct((b, MEM_LEN, X_WIDTH), BF16)
    spec = pl.BlockSpec((1, MEM_LEN, X_WIDTH), lambda i: (i, 0, 0))
    return pl.pallas_call(
        _memkv_kernel,
        grid=(b,),
        in_specs=[pl.BlockSpec((1, MEM_LEN, D_MODEL), lambda i: (i, 0, 0)),
                  _layer_spec(g, layer), _layer_spec(w, layer), _layer_spec(gk, layer)],
        out_specs=[spec, spec],
        out_shape=[out, out],
        compiler_params=_params(("arbitrary",)),
        name="mem_kv",
    )(mem, g, w, gk)


def _front_kernel(x_ref, cc_ref, ss_ref, kx_ref, vx_ref, gmix_ref, win_ref,
                  gqa_ref, wq_ref, gkva_ref, wk_ref, wvt_ref, gq_ref, gqs_ref, gk_ref, gks_ref,
                  gxq_ref, q_ref, k_ref, vt_ref, u_ref, yc_ref):
    tm = x_ref.shape[0]
    h = _rms(x_ref[...], gmix_ref[...]).astype(BF16)
    p = _dot(h, win_ref[...])
    o_ckv = Q_LORA
    o_kr = o_ckv + KV_LORA
    o_krs = o_kr + HEAD_PAD
    o_u = o_krs + HEAD_PAD
    o_xq = o_u + SSM_WIDTH
    cq = p[:, :o_ckv]
    ckv = p[:, o_ckv:o_kr]
    kr = p[:, o_kr:o_krs]
    krs = p[:, o_krs:o_u]
    u_ref[...] = p[:, o_u:o_xq].astype(BF16)
    xq = p[:, o_xq:]

    cqn = _rms(cq, gqa_ref[...]).astype(BF16)
    ckvn = _rms(ckv, gkva_ref[...]).astype(BF16)

    vt = _dot_nt(wvt_ref[...], ckvn)
    ones = jnp.ones((V_ROWS - D_V, TK), BF16)
    for hd in range(MLA_HEADS):
        for c in range(tm // TK):
            vt_ref[0, hd, c, :D_V, :] = vt[hd * D_V:(hd + 1) * D_V, c * TK:(c + 1) * TK].astype(BF16)
            vt_ref[0, hd, c, D_V:, :] = ones

    cc = cc_ref[...]
    ss = ss_ref[...]
    nq = MLA_HEADS * HEAD_PAD

    def inv_rms(a):
        return lax.rsqrt(jnp.sum(a * a, axis=-1, keepdims=True) * (1.0 / D_QK) + EPS)

    qa = _dot(cqn, wq_ref[...])
    gc = gq_ref[...] * cc
    gs = gqs_ref[...] * ss
    scale = D_QK ** -0.5 * math.log2(math.e)
    for hd in range(MLA_HEADS):
        a = qa[:, hd * HEAD_PAD:(hd + 1) * HEAD_PAD]
        a_s = qa[:, nq + hd * HEAD_PAD:nq + (hd + 1) * HEAD_PAD]
        q_ref[0, hd] = ((inv_rms(a) * scale) * (a * gc + a_s * gs)).astype(BF16)

    ka = _dot(ckvn, wk_ref[...])
    gc = gk_ref[...] * cc
    rot = krs * (gks_ref[...] * ss)
    for hd in range(MLA_HEADS):
        a = ka[:, hd * HEAD_PAD:(hd + 1) * HEAD_PAD] + kr
        k_ref[0, hd] = (inv_rms(a) * (a * gc + rot)).astype(BF16)

    xscale = X_HEAD_DIM ** -0.5
    for hd in range(X_HEADS):
        sl = slice(hd * X_HEAD_DIM, (hd + 1) * X_HEAD_DIM)
        qx = (_rms(xq[:, sl], gxq_ref[...]) * xscale).astype(BF16)
        s = _dot_nt(qx, kx_ref[0, :, sl])
        pm = jnp.exp(s - jnp.max(s, axis=-1, keepdims=True))
        l = jnp.sum(pm, axis=-1, keepdims=True)
        yc_ref[:, sl] = (_dot(pm.astype(BF16), vx_ref[0, :, sl]) / l).astype(BF16)


def _front(x2, tables, kx, vx, w, layer, bsz, seq):
    t = bsz * seq
    tm = TM_FRONT
    nb = seq // tm
    row = lambda c: pl.BlockSpec((tm, c), lambda i: (i, 0))
    hspec = pl.BlockSpec((1, MLA_HEADS, tm, HEAD_PAD), lambda i: (i // nb, 0, i % nb, 0))
    mspec = pl.BlockSpec((1, MEM_LEN, X_WIDTH), lambda i: (i // nb, 0, 0))
    consts = [w["gmix"], w["win"], w["gqa"], w["wq"], w["gkva"], w["wk"], w["wvt"],
              w["gq"], w["gqs"], w["gk"], w["gks"], w["gxq"]]
    hshape = jax.ShapeDtypeStruct((bsz, MLA_HEADS, seq, HEAD_PAD), BF16)
    tshape = jax.ShapeDtypeStruct((t, SSM_WIDTH), BF16)
    vtshape = jax.ShapeDtypeStruct((bsz, MLA_HEADS, seq // TK, V_ROWS, TK), BF16)
    vtspec = pl.BlockSpec((1, MLA_HEADS, tm // TK, V_ROWS, TK), lambda i: (i // nb, 0, i % nb, 0, 0))
    return pl.pallas_call(
        _front_kernel,
        grid=(t // tm,),
        in_specs=[row(D_MODEL), row(HEAD_PAD), row(HEAD_PAD), mspec, mspec]
                 + [_layer_spec(c, layer) for c in consts],
        out_specs=[hspec, hspec, vtspec, row(SSM_WIDTH), row(X_WIDTH)],
        out_shape=[hshape, hshape, vtshape, tshape, tshape],
        compiler_params=_params(("arbitrary",)),
        name="front",
    )(x2, *tables, kx, vx, *consts)


def _attn_kernel(q_ref, k_ref, vt_ref, o_ref, st_scr, m_scr, acc_scr):
    pair = pl.program_id(2)
    per = TQ // TK

    def scores(kb, slot, hh, qoff):
        r = pl.multiple_of(kb * TQ, TQ)
        st_scr[slot, hh] = _dot_nt(k_ref[0, hh, pl.ds(r, TQ), :], q_ref[0, hh, qoff:qoff + TQ, :])

    def consume(kb, slot, hh, masked):
        st = st_scr[slot, hh]
        if masked:
            key = lax.broadcasted_iota(jnp.int32, st.shape, 0)
            qry = lax.broadcasted_iota(jnp.int32, st.shape, 1)
            st = jnp.where(key <= qry, st, -jnp.inf)
        m_old = m_scr[hh]
        m_new = jnp.maximum(m_old, jnp.max(st, axis=0, keepdims=True))
        pm = jnp.exp2(st - m_new).astype(BF16)
        alpha = jnp.exp2(m_old - m_new)
        vt = jnp.concatenate([vt_ref[0, hh, kb * per + i] for i in range(per)], axis=1)
        acc_scr[hh] = alpha * acc_scr[hh] + _dot(vt, pm)
        m_scr[hh] = m_new

    def step(kb, slot, qoff, masked=False, ahead=None):
        for hh in range(ATTN_HEADS):
            if ahead is not None:
                scores(ahead[0], 1 - slot, hh, ahead[1])
            consume(kb, slot, hh, masked)

    def reset():
        m_scr[...] = jnp.full(m_scr.shape, -jnp.inf, F32)
        acc_scr[...] = jnp.zeros(acc_scr.shape, F32)

    def finish(qoff):
        outs = []
        for hh in range(ATTN_HEADS):
            a = acc_scr[hh]
            outs.append(a[:D_V] / a[D_V:D_V + 1])
        o_ref[0, qoff:qoff + TQ, :] = jnp.concatenate(outs, axis=0).T.astype(BF16)

    reset()
    for hh in range(ATTN_HEADS):
        scores(0, 0, hh, 0)

    def even_body(j, c):
        step(2 * j, 0, 0, ahead=(2 * j + 1, 0))
        step(2 * j + 1, 1, 0, ahead=(2 * j + 2, 0))
        return c

    lax.fori_loop(0, pair, even_body, 0)
    step(2 * pair, 0, 0, masked=True, ahead=(0, TQ))
    finish(0)

    reset()

    def odd_body(j, c):
        step(2 * j, 1, TQ, ahead=(2 * j + 1, TQ))
        step(2 * j + 1, 0, TQ, ahead=(2 * j + 2, TQ))
        return c

    lax.fori_loop(0, pair, odd_body, 0)
    step(2 * pair, 1, TQ, ahead=(2 * pair + 1, TQ))
    step(2 * pair + 1, 0, TQ, masked=True)
    finish(TQ)


def _attention(q, k, vt):
    bsz, nh, seq, _ = q.shape
    assert TQ % TK == 0
    nh_blk = ATTN_HEADS
    tq2 = 2 * TQ
    return pl.pallas_call(
        _attn_kernel,
        grid=(bsz, nh // nh_blk, seq // tq2),
        in_specs=[pl.BlockSpec((1, nh_blk, tq2, HEAD_PAD), lambda b, hg, i: (b, hg, i, 0)),
                  pl.BlockSpec((1, nh_blk, seq, HEAD_PAD), lambda b, hg, i: (b, hg, 0, 0)),
                  pl.BlockSpec((1, nh_blk, seq // TK, V_ROWS, TK), lambda b, hg, i: (b, hg, 0, 0, 0))],
        out_specs=pl.BlockSpec((1, tq2, nh_blk * D_V), lambda b, hg, i: (b, i, hg)),
        out_shape=jax.ShapeDtypeStruct((bsz, seq, nh * D_V), BF16),
        scratch_shapes=[pltpu.VMEM((2, nh_blk, TQ, TQ), F32), pltpu.VMEM((nh_blk, 1, TQ), F32),
                        pltpu.VMEM((nh_blk, V_ROWS, TQ), F32)],
        compiler_params=_params(("arbitrary", "arbitrary", "arbitrary")),
        name="mla_attention",
    )(q, k, vt)


def _s5_prep_kernel(lr_ref, li_ref, ldt_ref, lrc_ref, lic_ref, ldtc_ref, bre_ref, bim_ref,
                    ptab_ref, bbre_ref, bbim_ref):
    ns = SSM_NS
    dt = jnp.exp(ldt_ref[0])
    lr = lr_ref[0] * dt
    li = li_ref[0] * dt
    k = (lax.broadcasted_iota(jnp.int32, (S5_SEG, 1), 0) + 1).astype(F32)
    mag = jnp.exp(k * lr)
    ang = k * li
    p_re = mag * jnp.cos(ang)
    p_im = mag * jnp.sin(ang)
    for r in range(S5_SEG):
        rows = slice(r * SUBLANES, (r + 1) * SUBLANES)
        ptab_ref[0, rows, :ns] = jnp.broadcast_to(p_re[r:r + 1], (SUBLANES, ns))
        ptab_ref[0, rows, ns:] = jnp.broadcast_to(p_im[r:r + 1], (SUBLANES, ns))
    dtc = jnp.exp(ldtc_ref[0])
    lrc = lrc_ref[0]
    lic = lic_ref[0]
    magc = jnp.exp(lrc * dtc)
    e_re = magc * jnp.cos(lic * dtc) - 1.0
    e_im = magc * jnp.sin(lic * dtc)
    den = lrc * lrc + lic * lic
    f_re = (e_re * lrc + e_im * lic) / den
    f_im = (e_im * lrc - e_re * lic) / den
    bre = bre_ref[0]
    bim = bim_ref[0]
    bbre_ref[0] = f_re * bre - f_im * bim
    bbim_ref[0] = f_re * bim + f_im * bre


def _s5_prep(lam_re, lam_im, log_dt, b_re, b_im):
    depth = lam_re.shape[0]
    ns = SSM_NS
    ldt = jnp.broadcast_to(log_dt[:, :, None], lam_re.shape)
    rowv = lambda a: a.reshape(depth, 1, ns)
    colv = lambda a: a.reshape(depth, ns, 1)
    bcol = lambda a: a.reshape(depth, ns, SSM_GROUP_CH)
    spec = lambda s: pl.BlockSpec((1,) + s, lambda i: (i, 0, 0))
    return pl.pallas_call(
        _s5_prep_kernel,
        grid=(depth,),
        in_specs=[spec((1, ns))] * 3 + [spec((ns, 1))] * 3 + [spec((ns, SSM_GROUP_CH))] * 2,
        out_specs=[spec((S5_CHUNK, 2 * ns)),
                   spec((ns, SSM_GROUP_CH)), spec((ns, SSM_GROUP_CH))],
        out_shape=[jax.ShapeDtypeStruct((depth, S5_CHUNK, 2 * ns), F32),
                   jax.ShapeDtypeStruct((depth, ns, SSM_GROUP_CH), F32),
                   jax.ShapeDtypeStruct((depth, ns, SSM_GROUP_CH), F32)],
        compiler_params=_params(("arbitrary",)),
        name="s5_prep",
    )(rowv(lam_re), rowv(lam_im), rowv(ldt), colv(lam_re), colv(lam_im), colv(ldt),
      bcol(b_re), bcol(b_im))


def _s5_kernel(u_ref, bmat_ref, cre_ref, cim_ref, ptab_ref, d_ref, wglu_ref, bglu_ref,
               o_ref, s_scr, carry_scr, c_scr):
    ns = SSM_NS
    cw = S5_COLS
    nb = u_ref.shape[0]
    hw = SSM_WIDTH // S5_HALVES
    hs = ns // S5_HALVES

    @pl.when(pl.program_id(0) == 0)
    def _():
        carry_scr[...] = jnp.zeros(carry_scr.shape, F32)

    for b in range(nb):
        for hf in range(S5_HALVES):
            bu = _dot(u_ref[b, :, hf * hw:(hf + 1) * hw], bmat_ref[hf])
            s_scr[b, :, hf * hs:(hf + 1) * hs] = bu[:, :hs]
            s_scr[b, :, ns + hf * hs:ns + (hf + 1) * hs] = bu[:, hs:]

    cols = lambda cg: (slice(cg * cw, (cg + 1) * cw), slice(ns + cg * cw, ns + (cg + 1) * cw))
    ncg = ns // cw

    for b in range(nb):
        z = {}
        for i in range(S5_SEG):
            rows = slice(i * SUBLANES, (i + 1) * SUBLANES)
            for cg in range(ncg):
                re, im = cols(cg)
                if i == 0:
                    z[cg] = (s_scr[b, rows, re], s_scr[b, rows, im])
                    continue
                a_re = ptab_ref[:SUBLANES, re]
                a_im = ptab_ref[:SUBLANES, im]
                z_re, z_im = z[cg]
                n_re = a_re * z_re - a_im * z_im + s_scr[b, rows, re]
                n_im = a_re * z_im + a_im * z_re + s_scr[b, rows, im]
                s_scr[b, rows, re] = n_re
                s_scr[b, rows, im] = n_im
                z[cg] = (n_re, n_im)

        c = {}
        for cg in range(ncg):
            re, im = cols(cg)
            e_re, e_im = z[cg]
            p_re = ptab_ref[S5_CHUNK - 1:S5_CHUNK, re]
            p_im = ptab_ref[S5_CHUNK - 1:S5_CHUNK, im]
            cur_re = carry_scr[b, :, re]
            cur_im = carry_scr[b, :, im]
            for j in range(SUBLANES):
                c_scr[b, j:j + 1, re] = cur_re
                c_scr[b, j:j + 1, im] = cur_im
                n_re = e_re[j:j + 1, :] + p_re * cur_re - p_im * cur_im
                n_im = e_im[j:j + 1, :] + p_re * cur_im + p_im * cur_re
                cur_re, cur_im = n_re, n_im
            carry_scr[b, :, re] = cur_re
            carry_scr[b, :, im] = cur_im
            c[cg] = (c_scr[b, :, re], c_scr[b, :, im])

        for i in range(S5_SEG):
            rows = slice(i * SUBLANES, (i + 1) * SUBLANES)
            for cg in range(ncg):
                re, im = cols(cg)
                c_re, c_im = c[cg]
                q_re = ptab_ref[rows, re]
                q_im = ptab_ref[rows, im]
                s_scr[b, rows, re] = s_scr[b, rows, re] + q_re * c_re - q_im * c_im
                s_scr[b, rows, im] = s_scr[b, rows, im] + q_re * c_im + q_im * c_re

    for b in range(nb):
        ys = []
        for hf in range(S5_HALVES):
            s_re = s_scr[b, :, hf * hs:(hf + 1) * hs].astype(BF16)
            s_im = s_scr[b, :, ns + hf * hs:ns + (hf + 1) * hs].astype(BF16)
            ys.append(_dot(s_re, cre_ref[hf]) - _dot(s_im, cim_ref[hf]))
        y = jnp.concatenate(ys, axis=1) + d_ref[...] * u_ref[b].astype(F32)
        y = jax.nn.gelu(y)
        zz = _dot(y.astype(BF16), wglu_ref[...]) + bglu_ref[...]
        o_ref[b] = (y * jax.nn.sigmoid(zz)).astype(BF16)


def _s5(u_perm, bmat, cre, cim, ptab, d, wglu, bglu, layer):
    bsz, seq, _ = u_perm.shape
    ns = SSM_NS
    blk = pl.BlockSpec((bsz, S5_CHUNK, SSM_WIDTH), lambda j: (0, j, 0))
    consts = [bmat, cre, cim, ptab, d, wglu, bglu]
    return pl.pallas_call(
        _s5_kernel,
        grid=(seq // S5_CHUNK,),
        in_specs=[blk] + [_layer_spec(c, layer) for c in consts],
        out_specs=blk,
        out_shape=jax.ShapeDtypeStruct((bsz, seq, SSM_WIDTH), BF16),
        scratch_shapes=[pltpu.VMEM((bsz, S5_CHUNK, 2 * ns), F32), pltpu.VMEM((bsz, 1, 2 * ns), F32),
                        pltpu.VMEM((bsz, SUBLANES, 2 * ns), F32)],
        compiler_params=_params(("arbitrary",)),
        name="s5_scan",
    )(u_perm, *consts)


def _mix_kernel(x_ref, ya_ref, yb_ref, yc_ref, gmix_ref, wg_ref, bg_ref, woa_ref, wob_ref,
                woc_ref, wout_ref, g_ref, wu_ref, cw_ref, cb_ref, wd_ref, o_ref, carry_g, carry_v,
                *, blocks_per_seq):
    tm = x_ref.shape[0]

    @pl.when(pl.program_id(0) % blocks_per_seq == 0)
    def _():
        carry_g[...] = jnp.zeros(carry_g.shape, F32)
        carry_v[...] = jnp.zeros(carry_v.shape, F32)

    x = x_ref[...]
    h = _rms(x, gmix_ref[...]).astype(BF16)
    merged = None
    for br, (y_ref, wo_ref) in enumerate(((ya_ref, woa_ref), (yb_ref, wob_ref), (yc_ref, woc_ref))):
        sl = slice(br * D_MODEL, (br + 1) * D_MODEL)
        gate = jax.nn.sigmoid(_dot(h, wg_ref[:, sl]) + bg_ref[:, sl])
        term = gate * _dot(y_ref[...], wo_ref[...])
        merged = term if merged is None else merged + term
    x = x + _dot(merged.astype(BF16), wout_ref[...])

    h2 = _rms(x, g_ref[...]).astype(BF16)
    row = lax.broadcasted_iota(jnp.int32, (tm // 2, 1), 0)

    def up_proj(c):
        lo = c * FFN_CHUNK
        return (_dot(h2, wu_ref[:, lo:lo + FFN_CHUNK]),
                _dot(h2, wu_ref[:, D_FF + lo:D_FF + lo + FFN_CHUNK]))

    def conv(cs, up, off, carry):
        ws = slice(off + cs.start, off + cs.stop)
        prev = carry[:, cs]
        carry[:, cs] = up[tm - SUBLANES:, :]
        hr = tm // 2
        outs = []
        for h0 in (0, hr):
            part = up[h0:h0 + hr, :]
            p1 = prev[SUBLANES - 1:SUBLANES, :]
            p2 = prev[SUBLANES - 2:SUBLANES - 1, :]
            m1 = jnp.where(row == 0, p1, pltpu.roll(part, 1, 0))
            m2 = jnp.where(row == 0, p2, jnp.where(row == 1, p1, pltpu.roll(part, 2, 0)))
            outs.append(cw_ref[0:1, ws] * m2 + cw_ref[1:2, ws] * m1 + cw_ref[2:3, ws] * part
                        + cb_ref[:, ws])
            prev = part[hr - SUBLANES:, :]
        return jnp.concatenate(outs, axis=0)

    nchunk = D_FF // FFN_CHUNK
    acc = jnp.zeros((tm, D_MODEL), F32)
    ahead = 2
    ups = [up_proj(c) for c in range(ahead)]
    acts = []
    for c in range(nchunk):
        if c + ahead < nchunk:
            ups.append(up_proj(c + ahead))
        cs = slice(c * FFN_CHUNK, (c + 1) * FFN_CHUNK)
        gv = conv(cs, ups[c][0], 0, carry_g)
        vv = conv(cs, ups[c][1], D_FF, carry_v)
        acts.append((gv * jax.nn.sigmoid(gv) * vv).astype(BF16))
        if len(acts) == DOWN_GROUP or c + 1 == nchunk:
            lo = (c + 1 - len(acts)) * FFN_CHUNK
            act = acts[0] if len(acts) == 1 else jnp.concatenate(acts, axis=1)
            acc = acc + _dot(act, wd_ref[lo:(c + 1) * FFN_CHUNK, :])
            acts = []
    o_ref[...] = x + acc


def _mix(x2, ya, yb, yc, w, layer, seq):
    t = x2.shape[0]
    tm = TM_MIX
    row = lambda c: pl.BlockSpec((tm, c), lambda i: (i, 0))
    consts = [w["gmix"], w["wg"], w["bg"], w["woa"], w["wob"], w["woc"], w["wout"],
              w["gffn"], w["wu"], w["cw"], w["cb"], w["wd"]]
    return pl.pallas_call(
        functools.partial(_mix_kernel, blocks_per_seq=seq // tm),
        grid=(t // tm,),
        in_specs=[row(D_MODEL), row(MLA_WIDTH), row(SSM_WIDTH), row(X_WIDTH)]
                 + [_layer_spec(c, layer) for c in consts],
        out_specs=row(D_MODEL),
        out_shape=jax.ShapeDtypeStruct((t, D_MODEL), F32),
        scratch_shapes=[pltpu.VMEM((SUBLANES, D_FF), F32), pltpu.VMEM((SUBLANES, D_FF), F32)],
        compiler_params=_params(("arbitrary",)),
        name="mix",
    )(x2, ya, yb, yc, *consts)


def _block_diag(blocks):
    g, r, c = blocks.shape
    eye = jnp.eye(g, dtype=bool)
    full = jnp.where(eye[:, None, :, None], blocks[:, :, None, :], jnp.zeros((), blocks.dtype))
    return full.reshape(g * r, g * c)


def _pad_lanes(a, lo, width=HEAD_PAD):
    pad = [(0, 0)] * (a.ndim - 1) + [(lo, width - lo - a.shape[-1])]
    return jnp.pad(a, pad)


def _swap_rope(a):
    half = D_ROPE // 2
    lo = a[..., D_NOPE:D_NOPE + half]
    hi = a[..., D_NOPE + half:D_QK]
    return jnp.concatenate([jnp.zeros_like(a[..., :D_NOPE]), hi, lo,
                            jnp.zeros_like(a[..., D_QK:])], axis=-1)


def _stacked_weights(p):
    depth = p["w_in"].shape[0]
    w_in = p["w_in"]
    o = 0
    parts = {}
    for name, width in (("cq", Q_LORA), ("ckv", KV_LORA), ("kr", D_ROPE), ("u", SSM_WIDTH),
                        ("xq", X_WIDTH), ("g", N_BRANCH * D_MODEL)):
        parts[name] = w_in[:, :, o:o + width]
        o += width
    kr = _pad_lanes(parts["kr"], D_NOPE)
    win = jnp.concatenate([parts["cq"], parts["ckv"], kr, _swap_rope(kr),
                           parts["u"], parts["xq"]], axis=2).astype(BF16)
    wq = _pad_lanes(p["w_q_b"].reshape(depth, Q_LORA, MLA_HEADS, D_QK), 0)
    wq = jnp.concatenate([wq.reshape(depth, Q_LORA, -1),
                          _swap_rope(wq).reshape(depth, Q_LORA, -1)], axis=2)
    wkv = p["w_kv_b"].reshape(depth, KV_LORA, MLA_HEADS, D_NOPE + D_V)
    wk = _pad_lanes(wkv[..., :D_NOPE], 0).reshape(depth, KV_LORA, -1)
    wvt = wkv[..., D_NOPE:].reshape(depth, KV_LORA, MLA_WIDTH).transpose(0, 2, 1)
    row = lambda a: a.reshape(depth, 1, -1)
    gq = _pad_lanes(row(p["q_norm_g"]), 0)
    gk = _pad_lanes(row(p["k_norm_g"]), 0)
    shp = (depth, S5_HALVES, SSM_GROUPS // S5_HALVES, SSM_STATE, SSM_GROUP_CH)
    cshp = (depth, S5_HALVES, SSM_GROUPS // S5_HALVES, SSM_GROUP_CH, SSM_STATE)
    bd = jax.vmap(jax.vmap(_block_diag))
    return {
        "gmix": row(p["norm_mix_g"]),
        "win": win,
        "gqa": row(p["q_a_norm_g"]),
        "wq": wq.astype(BF16),
        "gkva": row(p["kv_a_norm_g"]),
        "wk": wk.astype(BF16),
        "wvt": wvt.astype(BF16),
        "gq": gq,
        "gqs": _swap_rope(gq),
        "gk": gk,
        "gks": _swap_rope(gk),
        "gxq": row(p["xq_norm_g"]),
        "wg": parts["g"].astype(BF16),
        "bg": row(p["b_gate"]),
        "woa": p["w_o_mla"].astype(BF16),
        "wob": p["w_o_ssm"].astype(BF16),
        "woc": p["w_o_cross"].astype(BF16),
        "wout": p["w_out"].astype(BF16),
        "gffn": row(p["norm_ffn_g"]),
        "wu": p["w_up"].astype(BF16),
        "cw": p["conv_w"],
        "cb": row(p["conv_b"]),
        "wd": p["w_down"].astype(BF16),
        "gmem": row(p["mem_norm_g"]),
        "wmem": p["w_mem_kv"].astype(BF16),
        "gxk": row(p["xk_norm_g"]),
        "bmat": jnp.concatenate(
            [bd(p["bb_re"].reshape(shp).transpose(0, 1, 2, 4, 3)),
             bd(p["bb_im"].reshape(shp).transpose(0, 1, 2, 4, 3))], axis=3).astype(BF16),
        "cre": bd(p["ssm_c_re"].reshape(cshp).transpose(0, 1, 2, 4, 3)).astype(BF16),
        "cim": bd(p["ssm_c_im"].reshape(cshp).transpose(0, 1, 2, 4, 3)).astype(BF16),
        "ssm_d": row(p["ssm_d"]),
        "wglu": p["w_glu"].astype(BF16),
        "bglu": row(p["b_glu"]),
    }


def _segment_major(a, bsz, seq):
    c = a.shape[-1]
    a = a.reshape(bsz, seq // S5_CHUNK, SUBLANES, S5_SEG, c)
    return a.transpose(0, 1, 3, 2, 4).reshape(bsz, seq, c)


def _time_major(a, bsz, seq):
    c = a.shape[-1]
    a = a.reshape(bsz, seq // S5_CHUNK, S5_SEG, SUBLANES, c)
    return a.transpose(0, 1, 3, 2, 4).reshape(bsz, seq, c)


def kernel(x, mem, positions, norm_mix_g, w_in, q_a_norm_g, w_q_b, kv_a_norm_g, w_kv_b, q_norm_g, k_norm_g, w_o_mla, ssm_lambda_re, ssm_lambda_im, ssm_log_dt, ssm_b_re, ssm_b_im, ssm_c_re, ssm_c_im, ssm_d, w_glu, b_glu, w_o_ssm, mem_norm_g, w_mem_kv, xq_norm_g, xk_norm_g, w_o_cross, b_gate, w_out, norm_ffn_g, w_up, conv_w, conv_b, w_down):
    bsz, seq, _ = x.shape
    t = bsz * seq
    assert seq % max(TM_FRONT, TM_MIX, 2 * TQ, S5_CHUNK) == 0 and t % (ROPE_ROWS * LANES // (D_ROPE // 2)) == 0

    inv_freq = ROPE_THETA ** (-jnp.arange(0, D_ROPE, 2, dtype=F32) / D_ROPE)
    tables = _rope_tables(positions, inv_freq)

    ptab, bb_re, bb_im = _s5_prep(ssm_lambda_re, ssm_lambda_im, ssm_log_dt, ssm_b_re, ssm_b_im)
    w = _stacked_weights(dict(
        norm_mix_g=norm_mix_g, w_in=w_in, q_a_norm_g=q_a_norm_g, w_q_b=w_q_b,
        kv_a_norm_g=kv_a_norm_g, w_kv_b=w_kv_b, q_norm_g=q_norm_g, k_norm_g=k_norm_g,
        w_o_mla=w_o_mla, w_o_ssm=w_o_ssm, w_o_cross=w_o_cross, b_gate=b_gate, w_out=w_out,
        norm_ffn_g=norm_ffn_g, w_up=w_up, conv_w=conv_w, conv_b=conv_b, w_down=w_down,
        xq_norm_g=xq_norm_g, mem_norm_g=mem_norm_g, w_mem_kv=w_mem_kv, xk_norm_g=xk_norm_g,
        bb_re=bb_re, bb_im=bb_im, ssm_c_re=ssm_c_re, ssm_c_im=ssm_c_im, ssm_d=ssm_d,
        w_glu=w_glu, b_glu=b_glu))

    x2 = x.reshape(t, D_MODEL)
    for i in range(DEPTH):
        kx, vx = _memkv(mem, w["gmem"], w["wmem"], w["gxk"], i)
        q, k, vt, u, yc = _front(x2, tables, kx, vx, w, i, bsz, seq)
        ya = _attention(q, k, vt).reshape(t, MLA_WIDTH)
        yb = _s5(_segment_major(u.reshape(bsz, seq, SSM_WIDTH), bsz, seq), w["bmat"], w["cre"],
                 w["cim"], ptab, w["ssm_d"], w["wglu"], w["bglu"], i)
        yb = _time_major(yb, bsz, seq).reshape(t, SSM_WIDTH)
        x2 = _mix(x2, ya, yb, yc, w, i, seq)
    return x2.reshape(bsz, seq, D_MODEL)
```

```python
import functools
import math

import jax
import jax.numpy as jnp
from jax import lax
from jax.experimental import pallas as pl
from jax.experimental.pallas import tpu as pltpu

F32 = jnp.float32
BF16 = jnp.bfloat16

D_MODEL = 1024
DEPTH = 2
MEM_LEN = 256
EPS = 1e-6
MLA_HEADS = 8
Q_LORA = 384
KV_LORA = 256
D_NOPE = 64
D_ROPE = 32
D_QK = D_NOPE + D_ROPE
D_V = 64
MLA_WIDTH = MLA_HEADS * D_V
ROPE_THETA = 10000.0
SSM_GROUPS = 32
SSM_GROUP_CH = 16
SSM_WIDTH = SSM_GROUPS * SSM_GROUP_CH
SSM_STATE = 64
SSM_NS = SSM_GROUPS * SSM_STATE
X_HEADS = 4
X_HEAD_DIM = 128
X_WIDTH = X_HEADS * X_HEAD_DIM
N_BRANCH = 3
D_FF = 2816
CONV_WIDTH = 3

LANES = 128
SUBLANES = 8
HEAD_PAD = LANES

TM_FRONT = 1024
TM_MIX = 512
FFN_CHUNK = 256
DOWN_GROUP = 4
TQ = 512
TK = 256
ATTN_HEADS = 4
V_ROWS = D_V + 16
S5_SEG = 64
S5_CHUNK = SUBLANES * S5_SEG
S5_COLS = 512
S5_HALVES = 2
ROPE_ROWS = 512

VMEM_LIMIT = 56 * 1024 * 1024


def _layer_spec(arr, layer):
    nd = arr.ndim - 1
    return pl.BlockSpec((None,) + arr.shape[1:], lambda *_: (layer,) + (0,) * nd,
                        pipeline_mode=pl.Buffered(1))


def _rms(x, g):
    return x * lax.rsqrt(jnp.mean(x * x, axis=-1, keepdims=True) + EPS) * g


def _dot(a, b):
    return jnp.dot(a, b, preferred_element_type=F32)


def _dot_nt(a, b):
    return lax.dot_general(a, b, (((1,), (1,)), ((), ())), preferred_element_type=F32)


def _params(sem, limit=VMEM_LIMIT):
    return pltpu.CompilerParams(dimension_semantics=sem, vmem_limit_bytes=limit)


def _rope_kernel(pos_ref, invf_ref, cos_ref, sin_ref, nsin_ref):
    ang = pos_ref[...].astype(F32) * invf_ref[...]
    s = jnp.sin(ang)
    cos_ref[...] = jnp.cos(ang)
    sin_ref[...] = s
    nsin_ref[...] = -s


def _rope_tables(positions, inv_freq):
    t = positions.size
    half = D_ROPE // 2
    per_row = LANES // half
    rows = t // per_row
    pos = jnp.repeat(positions.reshape(rows, per_row), half, axis=1)
    invf = jnp.tile(inv_freq, per_row).reshape(1, 1, LANES)
    out = jax.ShapeDtypeStruct((rows, LANES), F32)
    spec = pl.BlockSpec((ROPE_ROWS, LANES), lambda i: (i, 0))
    cos, sin, nsin = pl.pallas_call(
        _rope_kernel,
        grid=(rows // ROPE_ROWS,),
        in_specs=[spec, _layer_spec(invf, 0)],
        out_specs=[spec, spec, spec],
        out_shape=[out, out, out],
        compiler_params=_params(("arbitrary",)),
        name="rope_tables",
    )(pos, invf)
    cos, sin, nsin = (a.reshape(t, half) for a in (cos, sin, nsin))
    cc = jnp.concatenate([jnp.ones((t, D_NOPE), F32), cos, cos,
                          jnp.zeros((t, HEAD_PAD - D_QK), F32)], axis=1)
    ss = jnp.concatenate([jnp.zeros((t, D_NOPE), F32), nsin, sin,
                          jnp.zeros((t, HEAD_PAD - D_QK), F32)], axis=1)
    return cc, ss


def _memkv_kernel(mem_ref, g_ref, w_ref, gk_ref, k_ref, v_ref):
    m = _rms(mem_ref[0], g_ref[...]).astype(BF16)
    kv = _dot(m, w_ref[...])
    for h in range(X_HEADS):
        sl = slice(h * X_HEAD_DIM, (h + 1) * X_HEAD_DIM)
        k_ref[0, :, sl] = _rms(kv[:, sl], gk_ref[...]).astype(BF16)
    v_ref[0] = kv[:, X_WIDTH:].astype(BF16)


def _memkv(mem, g, w, gk, layer):
    b = mem.shape[0]
    out = jax.ShapeDtypeStruct((b, MEM_LEN, X_WIDTH), BF16)
    spec = pl.BlockSpec((1, MEM_LEN, X_WIDTH), lambda i: (i, 0, 0))
    return pl.pallas_call(
        _memkv_kernel,
        grid=(b,),
        in_specs=[pl.BlockSpec((1, MEM_LEN, D_MODEL), lambda i: (i, 0, 0)),
                  _layer_spec(g, layer), _layer_spec(w, layer), _layer_spec(gk, layer)],
        out_specs=[spec, spec],
        out_shape=[out, out],
        compiler_params=_params(("arbitrary",)),
        name="mem_kv",
    )(mem, g, w, gk)


def _front_kernel(x_ref, cc_ref, ss_ref, kx_ref, vx_ref, gmix_ref, win_ref,
                  gqa_ref, wq_ref, gkva_ref, wk_ref, wvt_ref, gq_ref, gqs_ref, gk_ref, gks_ref,
                  gxq_ref, q_ref, k_ref, vt_ref, u_ref, yc_ref):
    tm = x_ref.shape[0]
    h = _rms(x_ref[...], gmix_ref[...]).astype(BF16)
    p = _dot(h, win_ref[...])
    o_ckv = Q_LORA
    o_kr = o_ckv + KV_LORA
    o_krs = o_kr + HEAD_PAD
    o_u = o_krs + HEAD_PAD
    o_xq = o_u + SSM_WIDTH
    cq = p[:, :o_ckv]
    ckv = p[:, o_ckv:o_kr]
    kr = p[:, o_kr:o_krs]
    krs = p[:, o_krs:o_u]
    u_ref[...] = p[:, o_u:o_xq].astype(BF16)
    xq = p[:, o_xq:]

    cqn = _rms(cq, gqa_ref[...]).astype(BF16)
    ckvn = _rms(ckv, gkva_ref[...]).astype(BF16)

    vt = _dot_nt(wvt_ref[...], ckvn)
    ones = jnp.ones((V_ROWS - D_V, TK), BF16)
    for hd in range(MLA_HEADS):
        for c in range(tm // TK):
            vt_ref[0, hd, c, :D_V, :] = vt[hd * D_V:(hd + 1) * D_V, c * TK:(c + 1) * TK].astype(BF16)
            vt_ref[0, hd, c, D_V:, :] = ones

    cc = cc_ref[...]
    ss = ss_ref[...]
    nq = MLA_HEADS * HEAD_PAD

    def inv_rms(a):
        return lax.rsqrt(jnp.sum(a * a, axis=-1, keepdims=True) * (1.0 / D_QK) + EPS)

    qa = _dot(cqn, wq_ref[...])
    gc = gq_ref[...] * cc
    gs = gqs_ref[...] * ss
    scale = D_QK ** -0.5 * math.log2(math.e)
    for hd in range(MLA_HEADS):
        a = qa[:, hd * HEAD_PAD:(hd + 1) * HEAD_PAD]
        a_s = qa[:, nq + hd * HEAD_PAD:nq + (hd + 1) * HEAD_PAD]
        q_ref[0, hd] = ((inv_rms(a) * scale) * (a * gc + a_s * gs)).astype(BF16)

    ka = _dot(ckvn, wk_ref[...])
    gc = gk_ref[...] * cc
    rot = krs * (gks_ref[...] * ss)
    for hd in range(MLA_HEADS):
        a = ka[:, hd * HEAD_PAD:(hd + 1) * HEAD_PAD] + kr
        k_ref[0, hd] = (inv_rms(a) * (a * gc + rot)).astype(BF16)

    xscale = X_HEAD_DIM ** -0.5
    for hd in range(X_HEADS):
        sl = slice(hd * X_HEAD_DIM, (hd + 1) * X_HEAD_DIM)
        qx = (_rms(xq[:, sl], gxq_ref[...]) * xscale).astype(BF16)
        s = _dot_nt(qx, kx_ref[0, :, sl])
        pm = jnp.exp(s - jnp.max(s, axis=-1, keepdims=True))
        l = jnp.sum(pm, axis=-1, keepdims=True)
        yc_ref[:, sl] = (_dot(pm.astype(BF16), vx_ref[0, :, sl]) / l).astype(BF16)


def _front(x2, tables, kx, vx, w, layer, bsz, seq):
    t = bsz * seq
    tm = TM_FRONT
    nb = seq // tm
    row = lambda c: pl.BlockSpec((tm, c), lambda i: (i, 0))
    hspec = pl.BlockSpec((1, MLA_HEADS, tm, HEAD_PAD), lambda i: (i // nb, 0, i % nb, 0))
    mspec = pl.BlockSpec((1, MEM_LEN, X_WIDTH), lambda i: (i // nb, 0, 0))
    consts = [w["gmix"], w["win"], w["gqa"], w["wq"], w["gkva"], w["wk"], w["wvt"],
              w["gq"], w["gqs"], w["gk"], w["gks"], w["gxq"]]
    hshape = jax.ShapeDtypeStruct((bsz, MLA_HEADS, seq, HEAD_PAD), BF16)
    tshape = jax.ShapeDtypeStruct((t, SSM_WIDTH), BF16)
    vtshape = jax.ShapeDtypeStruct((bsz, MLA_HEADS, seq // TK, V_ROWS, TK), BF16)
    vtspec = pl.BlockSpec((1, MLA_HEADS, tm // TK, V_ROWS, TK), lambda i: (i // nb, 0, i % nb, 0, 0))
    return pl.pallas_call(
        _front_kernel,
        grid=(t // tm,),
        in_specs=[row(D_MODEL), row(HEAD_PAD), row(HEAD_PAD), mspec, mspec]
                 + [_layer_spec(c, layer) for c in consts],
        out_specs=[hspec, hspec, vtspec, row(SSM_WIDTH), row(X_WIDTH)],
        out_shape=[hshape, hshape, vtshape, tshape, tshape],
        compiler_params=_params(("arbitrary",)),
        name="front",
    )(x2, *tables, kx, vx, *consts)


def _attn_kernel(q_ref, k_ref, vt_ref, o_ref, st_scr, m_scr, acc_scr):
    pair = pl.program_id(2)
    per = TQ // TK

    def scores(kb, slot, hh, qoff):
        r = pl.multiple_of(kb * TQ, TQ)
        st_scr[slot, hh] = _dot_nt(k_ref[0, hh, pl.ds(r, TQ), :], q_ref[0, hh, qoff:qoff + TQ, :])

    def consume(kb, slot, hh, masked):
        st = st_scr[slot, hh]
        if masked:
            key = lax.broadcasted_iota(jnp.int32, st.shape, 0)
            qry = lax.broadcasted_iota(jnp.int32, st.shape, 1)
            st = jnp.where(key <= qry, st, -jnp.inf)
        m_old = m_scr[hh]
        m_new = jnp.maximum(m_old, jnp.max(st, axis=0, keepdims=True))
        pm = jnp.exp2(st - m_new).astype(BF16)
        alpha = jnp.exp2(m_old - m_new)
        vt = jnp.concatenate([vt_ref[0, hh, kb * per + i] for i in range(per)], axis=1)
        acc_scr[hh] = alpha * acc_scr[hh] + _dot(vt, pm)
        m_scr[hh] = m_new

    def step(kb, slot, qoff, masked=False, ahead=None):
        for hh in range(ATTN_HEADS):
            if ahead is not None:
                scores(ahead[0], 1 - slot, hh, ahead[1])
            consume(kb, slot, hh, masked)

    def reset():
        m_scr[...] = jnp.full(m_scr.shape, -jnp.inf, F32)
        acc_scr[...] = jnp.zeros(acc_scr.shape, F32)

    def finish(qoff):
        outs = []
        for hh in range(ATTN_HEADS):
            a = acc_scr[hh]
            outs.append(a[:D_V] / a[D_V:D_V + 1])
        o_ref[0, qoff:qoff + TQ, :] = jnp.concatenate(outs, axis=0).T.astype(BF16)

    reset()
    for hh in range(ATTN_HEADS):
        scores(0, 0, hh, 0)

    def even_body(j, c):
        step(2 * j, 0, 0, ahead=(2 * j + 1, 0))
        step(2 * j + 1, 1, 0, ahead=(2 * j + 2, 0))
        return c

    lax.fori_loop(0, pair, even_body, 0)
    step(2 * pair, 0, 0, masked=True, ahead=(0, TQ))
    finish(0)

    reset()

    def odd_body(j, c):
        step(2 * j, 1, TQ, ahead=(2 * j + 1, TQ))
        step(2 * j + 1, 0, TQ, ahead=(2 * j + 2, TQ))
        return c

    lax.fori_loop(0, pair, odd_body, 0)
    step(2 * pair, 1, TQ, ahead=(2 * pair + 1, TQ))
    step(2 * pair + 1, 0, TQ, masked=True)
    finish(TQ)


def _attention(q, k, vt):
    bsz, nh, seq, _ = q.shape
    assert TQ % TK == 0
    nh_blk = ATTN_HEADS
    tq2 = 2 * TQ
    return pl.pallas_call(
        _attn_kernel,
        grid=(bsz, nh // nh_blk, seq // tq2),
        in_specs=[pl.BlockSpec((1, nh_blk, tq2, HEAD_PAD), lambda b, hg, i: (b, hg, i, 0)),
                  pl.BlockSpec((1, nh_blk, seq, HEAD_PAD), lambda b, hg, i: (b, hg, 0, 0)),
                  pl.BlockSpec((1, nh_blk, seq // TK, V_ROWS, TK), lambda b, hg, i: (b, hg, 0, 0, 0))],
        out_specs=pl.BlockSpec((1, tq2, nh_blk * D_V), lambda b, hg, i: (b, i, hg)),
        out_shape=jax.ShapeDtypeStruct((bsz, seq, nh * D_V), BF16),
        scratch_shapes=[pltpu.VMEM((2, nh_blk, TQ, TQ), F32), pltpu.VMEM((nh_blk, 1, TQ), F32),
                        pltpu.VMEM((nh_blk, V_ROWS, TQ), F32)],
        compiler_params=_params(("arbitrary", "arbitrary", "arbitrary")),
        name="mla_attention",
    )(q, k, vt)


def _s5_prep_kernel(lr_ref, li_ref, ldt_ref, lrc_ref, lic_ref, ldtc_ref, bre_ref, bim_ref,
                    ptab_ref, bbre_ref, bbim_ref):
    ns = SSM_NS
    dt = jnp.exp(ldt_ref[0])
    lr = lr_ref[0] * dt
    li = li_ref[0] * dt
    k = (lax.broadcasted_iota(jnp.int32, (S5_SEG, 1), 0) + 1).astype(F32)
    mag = jnp.exp(k * lr)
    ang = k * li
    p_re = mag * jnp.cos(ang)
    p_im = mag * jnp.sin(ang)
    for r in range(S5_SEG):
        rows = slice(r * SUBLANES, (r + 1) * SUBLANES)
        ptab_ref[0, rows, :ns] = jnp.broadcast_to(p_re[r:r + 1], (SUBLANES, ns))
        ptab_ref[0, rows, ns:] = jnp.broadcast_to(p_im[r:r + 1], (SUBLANES, ns))
    dtc = jnp.exp(ldtc_ref[0])
    lrc = lrc_ref[0]
    lic = lic_ref[0]
    magc = jnp.exp(lrc * dtc)
    e_re = magc * jnp.cos(lic * dtc) - 1.0
    e_im = magc * jnp.sin(lic * dtc)
    den = lrc * lrc + lic * lic
    f_re = (e_re * lrc + e_im * lic) / den
    f_im = (e_im * lrc - e_re * lic) / den
    bre = bre_ref[0]
    bim = bim_ref[0]
    bbre_ref[0] = f_re * bre - f_im * bim
    bbim_ref[0] = f_re * bim + f_im * bre


def _s5_prep(lam_re, lam_im, log_dt, b_re, b_im):
    depth = lam_re.shape[0]
    ns = SSM_NS
    ldt = jnp.broadcast_to(log_dt[:, :, None], lam_re.shape)
    rowv = lambda a: a.reshape(depth, 1, ns)
    colv = lambda a: a.reshape(depth, ns, 1)
    bcol = lambda a: a.reshape(depth, ns, SSM_GROUP_CH)
    spec = lambda s: pl.BlockSpec((1,) + s, lambda i: (i, 0, 0))
    return pl.pallas_call(
        _s5_prep_kernel,
        grid=(depth,),
        in_specs=[spec((1, ns))] * 3 + [spec((ns, 1))] * 3 + [spec((ns, SSM_GROUP_CH))] * 2,
        out_specs=[spec((S5_CHUNK, 2 * ns)),
                   spec((ns, SSM_GROUP_CH)), spec((ns, SSM_GROUP_CH))],
        out_shape=[jax.ShapeDtypeStruct((depth, S5_CHUNK, 2 * ns), F32),
                   jax.ShapeDtypeStruct((depth, ns, SSM_GROUP_CH), F32),
                   jax.ShapeDtypeStruct((depth, ns, SSM_GROUP_CH), F32)],
        compiler_params=_params(("arbitrary",)),
        name="s5_prep",
    )(rowv(lam_re), rowv(lam_im), rowv(ldt), colv(lam_re), colv(lam_im), colv(ldt),
      bcol(b_re), bcol(b_im))


def _s5_kernel(u_ref, bmat_ref, cre_ref, cim_ref, ptab_ref, d_ref, wglu_ref, bglu_ref,
               o_ref, s_scr, carry_scr, c_scr):
    ns = SSM_NS
    cw = S5_COLS
    nb = u_ref.shape[0]
    hw = SSM_WIDTH // S5_HALVES
    hs = ns // S5_HALVES

    @pl.when(pl.program_id(0) == 0)
    def _():
        carry_scr[...] = jnp.zeros(carry_scr.shape, F32)

    for b in range(nb):
        for hf in range(S5_HALVES):
            bu = _dot(u_ref[b, :, hf * hw:(hf + 1) * hw], bmat_ref[hf])
            s_scr[b, :, hf * hs:(hf + 1) * hs] = bu[:, :hs]
            s_scr[b, :, ns + hf * hs:ns + (hf + 1) * hs] = bu[:, hs:]

    cols = lambda cg: (slice(cg * cw, (cg + 1) * cw), slice(ns + cg * cw, ns + (cg + 1) * cw))
    ncg = ns // cw

    for b in range(nb):
        z = {}
        for i in range(S5_SEG):
            rows = slice(i * SUBLANES, (i + 1) * SUBLANES)
            for cg in range(ncg):
                re, im = cols(cg)
                if i == 0:
                    z[cg] = (s_scr[b, rows, re], s_scr[b, rows, im])
                    continue
                a_re = ptab_ref[:SUBLANES, re]
                a_im = ptab_ref[:SUBLANES, im]
                z_re, z_im = z[cg]
                n_re = a_re * z_re - a_im * z_im + s_scr[b, rows, re]
                n_im = a_re * z_im + a_im * z_re + s_scr[b, rows, im]
                s_scr[b, rows, re] = n_re
                s_scr[b, rows, im] = n_im
                z[cg] = (n_re, n_im)

        c = {}
        for cg in range(ncg):
            re, im = cols(cg)
            e_re, e_im = z[cg]
            p_re = ptab_ref[S5_CHUNK - 1:S5_CHUNK, re]
            p_im = ptab_ref[S5_CHUNK - 1:S5_CHUNK, im]
            cur_re = carry_scr[b, :, re]
            cur_im = carry_scr[b, :, im]
            for j in range(SUBLANES):
                c_scr[b, j:j + 1, re] = cur_re
                c_scr[b, j:j + 1, im] = cur_im
                n_re = e_re[j:j + 1, :] + p_re * cur_re - p_im * cur_im
                n_im = e_im[j:j + 1, :] + p_re * cur_im + p_im * cur_re
                cur_re, cur_im = n_re, n_im
            carry_scr[b, :, re] = cur_re
            carry_scr[b, :, im] = cur_im
            c[cg] = (c_scr[b, :, re], c_scr[b, :, im])

        for i in range(S5_SEG):
            rows = slice(i * SUBLANES, (i + 1) * SUBLANES)
            for cg in range(ncg):
                re, im = cols(cg)
                c_re, c_im = c[cg]
                q_re = ptab_ref[rows, re]
                q_im = ptab_ref[rows, im]
                s_scr[b, rows, re] = s_scr[b, rows, re] + q_re * c_re - q_im * c_im
                s_scr[b, rows, im] = s_scr[b, rows, im] + q_re * c_im + q_im * c_re

    for b in range(nb):
        ys = []
        for hf in range(S5_HALVES):
            s_re = s_scr[b, :, hf * hs:(hf + 1) * hs].astype(BF16)
            s_im = s_scr[b, :, ns + hf * hs:ns + (hf + 1) * hs].astype(BF16)
            ys.append(_dot(s_re, cre_ref[hf]) - _dot(s_im, cim_ref[hf]))
        y = jnp.concatenate(ys, axis=1) + d_ref[...] * u_ref[b].astype(F32)
        y = jax.nn.gelu(y)
        zz = _dot(y.astype(BF16), wglu_ref[...]) + bglu_ref[...]
        o_ref[b] = (y * jax.nn.sigmoid(zz)).astype(BF16)


def _s5(u_perm, bmat, cre, cim, ptab, d, wglu, bglu, layer):
    bsz, seq, _ = u_perm.shape
    ns = SSM_NS
    blk = pl.BlockSpec((bsz, S5_CHUNK, SSM_WIDTH), lambda j: (0, j, 0))
    consts = [bmat, cre, cim, ptab, d, wglu, bglu]
    return pl.pallas_call(
        _s5_kernel,
        grid=(seq // S5_CHUNK,),
        in_specs=[blk] + [_layer_spec(c, layer) for c in consts],
        out_specs=blk,
        out_shape=jax.ShapeDtypeStruct((bsz, seq, SSM_WIDTH), BF16),
        scratch_shapes=[pltpu.VMEM((bsz, S5_CHUNK, 2 * ns), F32), pltpu.VMEM((bsz, 1, 2 * ns), F32),
                        pltpu.VMEM((bsz, SUBLANES, 2 * ns), F32)],
        compiler_params=_params(("arbitrary",)),
        name="s5_scan",
    )(u_perm, *consts)


def _mix_kernel(x_ref, ya_ref, yb_ref, yc_ref, gmix_ref, wg_ref, bg_ref, woa_ref, wob_ref,
                woc_ref, wout_ref, g_ref, wu_ref, cw_ref, cb_ref, wd_ref, o_ref, carry_g, carry_v,
                *, blocks_per_seq):
    tm = x_ref.shape[0]

    @pl.when(pl.program_id(0) % blocks_per_seq == 0)
    def _():
        carry_g[...] = jnp.zeros(carry_g.shape, F32)
        carry_v[...] = jnp.zeros(carry_v.shape, F32)

    x = x_ref[...]
    h = _rms(x, gmix_ref[...]).astype(BF16)
    merged = None
    for br, (y_ref, wo_ref) in enumerate(((ya_ref, woa_ref), (yb_ref, wob_ref), (yc_ref, woc_ref))):
        sl = slice(br * D_MODEL, (br + 1) * D_MODEL)
        gate = jax.nn.sigmoid(_dot(h, wg_ref[:, sl]) + bg_ref[:, sl])
        term = gate * _dot(y_ref[...], wo_ref[...])
        merged = term if merged is None else merged + term
    x = x + _dot(merged.astype(BF16), wout_ref[...])
    o_ref[...] = x

    h2 = _rms(x, g_ref[...]).astype(BF16)
    row = lax.broadcasted_iota(jnp.int32, (tm // 2, 1), 0)

    def up_proj(c):
        lo = c * FFN_CHUNK
        return (_dot(h2, wu_ref[:, lo:lo + FFN_CHUNK]),
                _dot(h2, wu_ref[:, D_FF + lo:D_FF + lo + FFN_CHUNK]))

    def conv(cs, up, off, carry):
        ws = slice(off + cs.start, off + cs.stop)
        prev = carry[:, cs]
        carry[:, cs] = up[tm - SUBLANES:, :]
        hr = tm // 2
        outs = []
        for h0 in (0, hr):
            part = up[h0:h0 + hr, :]
            p1 = prev[SUBLANES - 1:SUBLANES, :]
            p2 = prev[SUBLANES - 2:SUBLANES - 1, :]
            m1 = jnp.where(row == 0, p1, pltpu.roll(part, 1, 0))
            m2 = jnp.where(row == 0, p2, jnp.where(row == 1, p1, pltpu.roll(part, 2, 0)))
            outs.append(cw_ref[0:1, ws] * m2 + cw_ref[1:2, ws] * m1 + cw_ref[2:3, ws] * part
                        + cb_ref[:, ws])
            prev = part[hr - SUBLANES:, :]
        return jnp.concatenate(outs, axis=0)

    nchunk = D_FF // FFN_CHUNK
    acc = jnp.zeros((tm, D_MODEL), F32)
    ahead = 2
    ups = [up_proj(c) for c in range(ahead)]
    acts = []
    for c in range(nchunk):
        if c + ahead < nchunk:
            ups.append(up_proj(c + ahead))
        cs = slice(c * FFN_CHUNK, (c + 1) * FFN_CHUNK)
        gv = conv(cs, ups[c][0], 0, carry_g)
        vv = conv(cs, ups[c][1], D_FF, carry_v)
        acts.append((gv * jax.nn.sigmoid(gv) * vv).astype(BF16))
        if len(acts) == DOWN_GROUP or c + 1 == nchunk:
            lo = (c + 1 - len(acts)) * FFN_CHUNK
            act = acts[0] if len(acts) == 1 else jnp.concatenate(acts, axis=1)
            acc = acc + _dot(act, wd_ref[lo:(c + 1) * FFN_CHUNK, :])
            acts = []
    o_ref[...] = o_ref[...] + acc


def _mix(x2, ya, yb, yc, w, layer, seq):
    t = x2.shape[0]
    tm = TM_MIX
    row = lambda c: pl.BlockSpec((tm, c), lambda i: (i, 0))
    consts = [w["gmix"], w["wg"], w["bg"], w["woa"], w["wob"], w["woc"], w["wout"],
              w["gffn"], w["wu"], w["cw"], w["cb"], w["wd"]]
    return pl.pallas_call(
        functools.partial(_mix_kernel, blocks_per_seq=seq // tm),
        grid=(t // tm,),
        in_specs=[row(D_MODEL), row(MLA_WIDTH), row(SSM_WIDTH), row(X_WIDTH)]
                 + [_layer_spec(c, layer) for c in consts],
        out_specs=row(D_MODEL),
        out_shape=jax.ShapeDtypeStruct((t, D_MODEL), F32),
        scratch_shapes=[pltpu.VMEM((SUBLANES, D_FF), F32), pltpu.VMEM((SUBLANES, D_FF), F32)],
        compiler_params=_params(("arbitrary",)),
        name="mix",
    )(x2, ya, yb, yc, *consts)


def _block_diag(blocks):
    g, r, c = blocks.shape
    eye = jnp.eye(g, dtype=bool)
    full = jnp.where(eye[:, None, :, None], blocks[:, :, None, :], jnp.zeros((), blocks.dtype))
    return full.reshape(g * r, g * c)


def _pad_lanes(a, lo, width=HEAD_PAD):
    pad = [(0, 0)] * (a.ndim - 1) + [(lo, width - lo - a.shape[-1])]
    return jnp.pad(a, pad)


def _swap_rope(a):
    half = D_ROPE // 2
    lo = a[..., D_NOPE:D_NOPE + half]
    hi = a[..., D_NOPE + half:D_QK]
    return jnp.concatenate([jnp.zeros_like(a[..., :D_NOPE]), hi, lo,
                            jnp.zeros_like(a[..., D_QK:])], axis=-1)


def _stacked_weights(p):
    depth = p["w_in"].shape[0]
    w_in = p["w_in"]
    o = 0
    parts = {}
    for name, width in (("cq", Q_LORA), ("ckv", KV_LORA), ("kr", D_ROPE), ("u", SSM_WIDTH),
                        ("xq", X_WIDTH), ("g", N_BRANCH * D_MODEL)):
        parts[name] = w_in[:, :, o:o + width]
        o += width
    kr = _pad_lanes(parts["kr"], D_NOPE)
    win = jnp.concatenate([parts["cq"], parts["ckv"], kr, _swap_rope(kr),
                           parts["u"], parts["xq"]], axis=2).astype(BF16)
    wq = _pad_lanes(p["w_q_b"].reshape(depth, Q_LORA, MLA_HEADS, D_QK), 0)
    wq = jnp.concatenate([wq.reshape(depth, Q_LORA, -1),
                          _swap_rope(wq).reshape(depth, Q_LORA, -1)], axis=2)
    wkv = p["w_kv_b"].reshape(depth, KV_LORA, MLA_HEADS, D_NOPE + D_V)
    wk = _pad_lanes(wkv[..., :D_NOPE], 0).reshape(depth, KV_LORA, -1)
    wvt = wkv[..., D_NOPE:].reshape(depth, KV_LORA, MLA_WIDTH).transpose(0, 2, 1)
    row = lambda a: a.reshape(depth, 1, -1)
    gq = _pad_lanes(row(p["q_norm_g"]), 0)
    gk = _pad_lanes(row(p["k_norm_g"]), 0)
    shp = (depth, S5_HALVES, SSM_GROUPS // S5_HALVES, SSM_STATE, SSM_GROUP_CH)
    cshp = (depth, S5_HALVES, SSM_GROUPS // S5_HALVES, SSM_GROUP_CH, SSM_STATE)
    bd = jax.vmap(jax.vmap(_block_diag))
    return {
        "gmix": row(p["norm_mix_g"]),
        "win": win,
        "gqa": row(p["q_a_norm_g"]),
        "wq": wq.astype(BF16),
        "gkva": row(p["kv_a_norm_g"]),
        "wk": wk.astype(BF16),
        "wvt": wvt.astype(BF16),
        "gq": gq,
        "gqs": _swap_rope(gq),
        "gk": gk,
        "gks": _swap_rope(gk),
        "gxq": row(p["xq_norm_g"]),
        "wg": parts["g"].astype(BF16),
        "bg": row(p["b_gate"]),
        "woa": p["w_o_mla"].astype(BF16),
        "wob": p["w_o_ssm"].astype(BF16),
        "woc": p["w_o_cross"].astype(BF16),
        "wout": p["w_out"].astype(BF16),
        "gffn": row(p["norm_ffn_g"]),
        "wu": p["w_up"].astype(BF16),
        "cw": p["conv_w"],
        "cb": row(p["conv_b"]),
        "wd": p["w_down"].astype(BF16),
        "gmem": row(p["mem_norm_g"]),
        "wmem": p["w_mem_kv"].astype(BF16),
        "gxk": row(p["xk_norm_g"]),
        "bmat": jnp.concatenate(
            [bd(p["bb_re"].reshape(shp).transpose(0, 1, 2, 4, 3)),
             bd(p["bb_im"].reshape(shp).transpose(0, 1, 2, 4, 3))], axis=3).astype(BF16),
        "cre": bd(p["ssm_c_re"].reshape(cshp).transpose(0, 1, 2, 4, 3)).astype(BF16),
        "cim": bd(p["ssm_c_im"].reshape(cshp).transpose(0, 1, 2, 4, 3)).astype(BF16),
        "ssm_d": row(p["ssm_d"]),
        "wglu": p["w_glu"].astype(BF16),
        "bglu": row(p["b_glu"]),
    }


def _segment_major(a, bsz, seq):
    c = a.shape[-1]
    a = a.reshape(bsz, seq // S5_CHUNK, SUBLANES, S5_SEG, c)
    return a.transpose(0, 1, 3, 2, 4).reshape(bsz, seq, c)


def _time_major(a, bsz, seq):
    c = a.shape[-1]
    a = a.reshape(bsz, seq // S5_CHUNK, S5_SEG, SUBLANES, c)
    return a.transpose(0, 1, 3, 2, 4).reshape(bsz, seq, c)


def kernel(x, mem, positions, norm_mix_g, w_in, q_a_norm_g, w_q_b, kv_a_norm_g, w_kv_b, q_norm_g, k_norm_g, w_o_mla, ssm_lambda_re, ssm_lambda_im, ssm_log_dt, ssm_b_re, ssm_b_im, ssm_c_re, ssm_c_im, ssm_d, w_glu, b_glu, w_o_ssm, mem_norm_g, w_mem_kv, xq_norm_g, xk_norm_g, w_o_cross, b_gate, w_out, norm_ffn_g, w_up, conv_w, conv_b, w_down):
    bsz, seq, _ = x.shape
    t = bsz * seq
    assert seq % max(TM_FRONT, TM_MIX, 2 * TQ, S5_CHUNK) == 0 and t % (ROPE_ROWS * LANES // (D_ROPE // 2)) == 0

    inv_freq = ROPE_THETA ** (-jnp.arange(0, D_ROPE, 2, dtype=F32) / D_ROPE)
    tables = _rope_tables(positions, inv_freq)

    ptab, bb_re, bb_im = _s5_prep(ssm_lambda_re, ssm_lambda_im, ssm_log_dt, ssm_b_re, ssm_b_im)
    w = _stacked_weights(dict(
        norm_mix_g=norm_mix_g, w_in=w_in, q_a_norm_g=q_a_norm_g, w_q_b=w_q_b,
        kv_a_norm_g=kv_a_norm_g, w_kv_b=w_kv_b, q_norm_g=q_norm_g, k_norm_g=k_norm_g,
        w_o_mla=w_o_mla, w_o_ssm=w_o_ssm, w_o_cross=w_o_cross, b_gate=b_gate, w_out=w_out,
        norm_ffn_g=norm_ffn_g, w_up=w_up, conv_w=conv_w, conv_b=conv_b, w_down=w_down,
        xq_norm_g=xq_norm_g, mem_norm_g=mem_norm_g, w_mem_kv=w_mem_kv, xk_norm_g=xk_norm_g,
        bb_re=bb_re, bb_im=bb_im, ssm_c_re=ssm_c_re, ssm_c_im=ssm_c_im, ssm_d=ssm_d,
        w_glu=w_glu, b_glu=b_glu))

    x2 = x.reshape(t, D_MODEL)
    for i in range(DEPTH):
        kx, vx = _memkv(mem, w["gmem"], w["wmem"], w["gxk"], i)
        q, k, vt, u, yc = _front(x2, tables, kx, vx, w, i, bsz, seq)
        ya = _attention(q, k, vt).reshape(t, MLA_WIDTH)
        yb = _s5(_segment_major(u.reshape(bsz, seq, SSM_WIDTH), bsz, seq), w["bmat"], w["cre"],
                 w["cim"], ptab, w["ssm_d"], w["wglu"], w["bglu"], i)
        yb = _time_major(yb, bsz, seq).reshape(t, SSM_WIDTH)
        x2 = _mix(x2, ya, yb, yc, w, i, seq)
    return x2.reshape(bsz, seq, D_MODEL)
```
